```python
import math
import jax, jax.numpy as jnp
from jax import lax
import numpy as np

D_MODEL = 1024
BATCH = 16
SEQ = 4096
DEPTH = 4

D_MIX = D_MODEL
D_HYENA = D_MIX // 2
D_FNET = D_MIX - D_HYENA
FNET_GROUPS = 8
FNET_GROUP_DIM = D_FNET // FNET_GROUPS
HYENA_ORDER = 2
D_IN_PROJ = (HYENA_ORDER + 1) * D_HYENA + D_FNET
SHORT_CONV = 3
FILTER_BANDS = 16
FILTER_EMB = 2 * FILTER_BANDS + 1
FILTER_HIDDEN = 64
N_DIRS = 2
DECAY_TARGET = 1e-2
FAST_DECAY_PCT = 0.3
SLOW_DECAY_PCT = 1.5
D_FF = 2816
FFN_CONV = 3
N_MOD = 6
NORM_EPS = 1e-6
FILTER_EPS = 1e-6

kernel_name = "hyena_fnet_hybrid_encoder"


def _rmsnorm(x, g):
    xf = x.astype(jnp.float32)
    y = xf * lax.rsqrt(jnp.mean(xf * xf, axis=-1, keepdims=True) + NORM_EPS)
    return (y * g.astype(jnp.float32)).astype(x.dtype)


def _centred_dwconv(u, w, b):
    width = w.shape[0]
    half = width // 2
    L = u.shape[1]
    up = jnp.pad(u, ((0, 0), (half, half), (0, 0)))
    y = b
    for k in range(width):
        y = y + up[:, k:k + L] * w[k]
    return y


def _filter_pos_features(L):
    pos = jnp.arange(L, dtype=jnp.float32)
    t = pos / max(L - 1, 1)
    bands = jnp.linspace(1e-4, FILTER_BANDS - 1, FILTER_BANDS, dtype=jnp.float32)
    ang = (2.0 * math.pi / L) * pos[:, None] * bands[None, :]
    z = jnp.concatenate([t[:, None], jnp.cos(ang), -jnp.sin(ang)], axis=-1)
    return z, t


def _hyena_filters_freq(L, w1, b1, w2, b2, freq, w3):
    f32 = jnp.float32
    z, t = _filter_pos_features(L)
    fr = freq.astype(f32)
    h = jnp.sin(fr * (z @ w1.astype(f32) + b1.astype(f32)))
    h = jnp.sin(fr * (h @ w2.astype(f32) + b2.astype(f32)))
    h = (h @ w3.astype(f32)).reshape(L, HYENA_ORDER, N_DIRS, D_HYENA)
    min_decay = math.log(DECAY_TARGET) / SLOW_DECAY_PCT
    max_decay = math.log(DECAY_TARGET) / FAST_DECAY_PCT
    deltas = jnp.linspace(min_decay, max_decay, D_HYENA, dtype=f32)
    decay = jnp.exp(-t[:, None] * jnp.abs(deltas)[None, :])
    h = h * decay[:, None, None, :]
    fwd = h[:, :, 0]
    bwd = h[:, :, 1]
    k = jnp.concatenate([fwd, jnp.zeros((1, HYENA_ORDER, D_HYENA), f32), bwd[:0:-1]], axis=0)
    k = k / (jnp.sum(jnp.abs(k), axis=0, keepdims=True) + FILTER_EPS)
    return jnp.fft.rfft(k, axis=0)


def _bidir_long_conv(u, k_f):
    L = u.shape[1]
    u_f = jnp.fft.rfft(u.astype(jnp.float32), n=2 * L, axis=1)
    y = jnp.fft.irfft(u_f * k_f[None], n=2 * L, axis=1)[:, :L]
    return y.astype(u.dtype)


def _hyena_mixer(p, short_w, short_b, k_f, d_skip):
    u = _centred_dwconv(p, short_w, short_b)
    v = u[..., :D_HYENA]
    gates = [u[..., (n + 1) * D_HYENA:(n + 2) * D_HYENA] for n in range(HYENA_ORDER)]
    z = v
    for n in range(HYENA_ORDER):
        z = gates[n] * (_bidir_long_conv(z, k_f[:, n]) + d_skip[n] * z)
    return z


def _fnet_mixer(p, w_grp):
    B, L, _ = p.shape
    r = p.reshape(B, L, FNET_GROUPS, FNET_GROUP_DIM).astype(jnp.float32)
    r = jnp.fft.fft2(r, axes=(1, 3), norm="ortho").real.astype(p.dtype)
    y = jnp.einsum("blgc,gcd->blgd", r, w_grp)
    return y.reshape(B, L, D_FNET)


def setup_inputs(seed: int = 0) -> dict:
    key = jax.random.key(seed)
    ks = jax.random.split(key, 24)
    f32 = jnp.float32
    nrm = lambda k, shape, s: (jax.random.normal(k, shape, f32) * s)
    return {
        "x": nrm(ks[0], (BATCH, SEQ, D_MODEL), 1.0),
        "c": nrm(ks[1], (BATCH, D_MODEL), 1.0),
        "ada_w": nrm(ks[2], (DEPTH, D_MODEL, N_MOD * D_MODEL), D_MODEL ** -0.5),
        "ada_b": nrm(ks[3], (DEPTH, N_MOD * D_MODEL), 0.02),
        "g_mix_pre": 1.0 + nrm(ks[4], (DEPTH, D_MODEL), 0.05),
        "g_mix_post": 1.0 + nrm(ks[5], (DEPTH, D_MODEL), 0.05),
        "w_in": nrm(ks[6], (DEPTH, D_MODEL, D_IN_PROJ), D_MODEL ** -0.5),
        "short_w": nrm(ks[7], (DEPTH, SHORT_CONV, (HYENA_ORDER + 1) * D_HYENA), SHORT_CONV ** -0.5),
        "short_b": nrm(ks[8], (DEPTH, (HYENA_ORDER + 1) * D_HYENA), 0.02),
        "filt_w1": nrm(ks[9], (DEPTH, FILTER_EMB, FILTER_HIDDEN), FILTER_EMB ** -0.5),
        "filt_b1": nrm(ks[10], (DEPTH, FILTER_HIDDEN), 0.02),
        "filt_w2": nrm(ks[11], (DEPTH, FILTER_HIDDEN, FILTER_HIDDEN), FILTER_HIDDEN ** -0.5),
        "filt_b2": nrm(ks[12], (DEPTH, FILTER_HIDDEN), 0.02),
        "filt_freq": 1.0 + nrm(ks[13], (DEPTH, FILTER_HIDDEN), 0.05),
        "filt_w3": nrm(ks[14], (DEPTH, FILTER_HIDDEN, HYENA_ORDER * N_DIRS * D_HYENA), FILTER_HIDDEN ** -0.5),
        "hyena_d": nrm(ks[15], (DEPTH, HYENA_ORDER, D_HYENA), 0.5),
        "fnet_w": nrm(ks[16], (DEPTH, FNET_GROUPS, FNET_GROUP_DIM, FNET_GROUP_DIM), FNET_GROUP_DIM ** -0.5),
        "w_out": nrm(ks[17], (DEPTH, D_MIX, D_MODEL), D_MIX ** -0.5),
        "g_ffn_pre": 1.0 + nrm(ks[18], (DEPTH, D_MODEL), 0.05),
        "g_ffn_post": 1.0 + nrm(ks[19], (DEPTH, D_MODEL), 0.05),
        "w_up": nrm(ks[20], (DEPTH, D_MODEL, 2 * D_FF), D_MODEL ** -0.5),
        "dw_w": nrm(ks[21], (DEPTH, FFN_CONV, 2 * D_FF), FFN_CONV ** -0.5),
        "dw_b": nrm(ks[22], (DEPTH, 2 * D_FF), 0.02),
        "w_down": nrm(ks[23], (DEPTH, D_FF, D_MODEL), D_FF ** -0.5),
    }


def reference(x, c, ada_w, ada_b, g_mix_pre, g_mix_post, w_in, short_w, short_b,
              filt_w1, filt_b1, filt_w2, filt_b2, filt_freq, filt_w3, hyena_d,
              fnet_w, w_out, g_ffn_pre, g_ffn_post, w_up, dw_w, dw_b, w_down):
    L = x.shape[1]
    c_act = jax.nn.silu(c.astype(jnp.float32))
    for l in range(DEPTH):
        mod = (c_act @ ada_w[l].astype(jnp.float32) + ada_b[l].astype(jnp.float32)).astype(x.dtype)
        shift_m, scale_m, gate_m, shift_f, scale_f, gate_f = jnp.split(mod[:, None, :], N_MOD, axis=-1)

        h = _rmsnorm(x, g_mix_pre[l]) * (1.0 + scale_m) + shift_m
        p = jnp.einsum("bld,de->ble", h, w_in[l])
        k_f = _hyena_filters_freq(L, filt_w1[l], filt_b1[l], filt_w2[l], filt_b2[l],
                                  filt_freq[l], filt_w3[l])
        y_hyena = _hyena_mixer(p[..., :(HYENA_ORDER + 1) * D_HYENA], short_w[l], short_b[l],
                               k_f, hyena_d[l])
        y_fnet = _fnet_mixer(p[..., (HYENA_ORDER + 1) * D_HYENA:], fnet_w[l])
        y = jnp.einsum("ble,ed->bld", jnp.concatenate([y_hyena, y_fnet], axis=-1), w_out[l])
        x = x + gate_m * _rmsnorm(y, g_mix_post[l])

        h = _rmsnorm(x, g_ffn_pre[l]) * (1.0 + scale_f) + shift_f
        u = _centred_dwconv(jnp.einsum("bld,df->blf", h, w_up[l]), dw_w[l], dw_b[l])
        a, b = jnp.split(u, 2, axis=-1)
        y = jnp.einsum("blf,fd->bld", jax.nn.gelu(a, approximate=False) * b, w_down[l])
        x = x + gate_f * _rmsnorm(y, g_ffn_post[l])
    return x
```

```python
import functools
import math

import numpy as np
import jax
import jax.numpy as jnp
from jax import lax
from jax.experimental import pallas as pl
from jax.experimental.pallas import tpu as pltpu

F32 = jnp.float32
BF16 = jnp.bfloat16

LANES = 128
SUBLANES = 8
HALO = SUBLANES
VMEM_LIMIT = 56 * 1024 * 1024

FNET_GROUPS = 8
HYENA_ORDER = 2
N_DIRS = 2
FILTER_BANDS = 16
FILTER_HIDDEN = 64
DECAY_TARGET = 1e-2
FAST_DECAY_PCT = 0.3
SLOW_DECAY_PCT = 1.5
N_MOD = 6
NORM_EPS = 1e-6
FILTER_EPS = 1e-6

CONV_N2 = 128
FNET_NB = 64
SLAB_PAD = 8


def _params(sem, vmem=VMEM_LIMIT):
    return pltpu.CompilerParams(dimension_semantics=sem, vmem_limit_bytes=vmem)


def _const_spec(shape):
    nd = len(shape)
    return pl.BlockSpec(shape, lambda *_: (0,) * nd, pipeline_mode=pl.Buffered(1))


@functools.lru_cache(maxsize=None)
def _conv_tables(L):
    n2n = CONV_N2
    nh = L // n2n
    n1n = 2 * nh
    N = 2 * L
    k1 = np.arange(n1n)[:, None]
    n1 = np.arange(nh)[None, :]
    tf = np.zeros((n2n, 2 * n1n, 2 * nh), np.float64)
    for n2 in range(n2n):
        th = 2.0 * np.pi * ((k1 * (n2n * n1 + n2)) % N) / N
        c, s = np.cos(th), np.sin(th)
        tf[n2] = np.block([[c, s], [-s, c]])
    ti = np.transpose(tf, (0, 2, 1))
    k2 = np.arange(n2n)[:, None]
    m = np.arange(n2n)[None, :]
    th2 = 2.0 * np.pi * ((k2 * m) % n2n) / n2n
    c2, s2 = np.cos(th2), np.sin(th2)
    m2 = np.block([[c2, s2], [-s2, c2]])
    return (jnp.asarray(tf.reshape(n2n * 2 * n1n, 2 * nh), BF16),
            jnp.asarray(ti.reshape(n2n * 2 * nh, 2 * n1n), BF16),
            jnp.asarray(m2, BF16), jnp.asarray(m2.T, BF16))


@functools.lru_cache(maxsize=None)
def _fnet_tables(L, cg):
    nb = FNET_NB
    na = L // nb
    scale = 1.0 / math.sqrt(L * cg)
    j = np.arange(cg)
    thc = 2.0 * np.pi * ((j[:, None] * j[None, :]) % cg) / cg
    reps = LANES // cg
    cc = np.kron(np.eye(reps), np.cos(thc)) * scale
    sc = np.kron(np.eye(reps), np.sin(thc)) * scale
    cs = np.concatenate([cc, -sc], axis=1)
    k1 = np.arange(na)[:, None]
    n1 = np.arange(na)[None, :]
    tf = np.zeros((nb, 2 * na, 2 * na), np.float64)
    for n2 in range(nb):
        th = 2.0 * np.pi * ((k1 * (nb * n1 + n2)) % L) / L
        c, s = np.cos(th), np.sin(th)
        tf[n2] = np.block([[c, s], [-s, c]])
    k2 = np.arange(nb)[:, None]
    m = np.arange(nb)[None, :]
    th2 = 2.0 * np.pi * ((k2 * m) % nb) / nb
    m4 = np.concatenate([np.cos(th2), np.sin(th2)], axis=1)
    return (jnp.asarray(cs, BF16), jnp.asarray(tf.reshape(nb * 2 * na, 2 * na), BF16),
            jnp.asarray(m4, BF16))


@functools.lru_cache(maxsize=None)
def _filter_features(L):
    pos = np.arange(L, dtype=np.float32)
    bands = np.linspace(1e-4, FILTER_BANDS - 1, FILTER_BANDS, dtype=np.float32)
    t = pos / np.float32(max(L - 1, 1))
    ang = np.float32(2.0 * math.pi / L) * pos[:, None] * bands[None, :]
    z = np.concatenate([t[:, None], np.cos(ang), -np.sin(ang)], axis=-1).astype(np.float32)
    zp = np.zeros((L, LANES), np.float32)
    zp[:, :z.shape[1]] = z
    return jnp.asarray(zp)


def _decay_rates(dh):
    min_decay = math.log(DECAY_TARGET) / SLOW_DECAY_PCT
    max_decay = math.log(DECAY_TARGET) / FAST_DECAY_PCT
    return jnp.abs(jnp.linspace(min_decay, max_decay, dh, dtype=F32)).reshape(1, dh)


def _modnorm(x, g, scale, shift):
    ms = jnp.mean(x * x, axis=-1, keepdims=True)
    return (x * lax.rsqrt(ms + NORM_EPS) * g) * (1.0 + scale) + shift


def _rmsnorm(y, g):
    ms = jnp.mean(y * y, axis=-1, keepdims=True)
    return y * lax.rsqrt(ms + NORM_EPS) * g


def _halo_rows(xm_ref, xp_ref, xn_ref, g, scale, shift, first, last):
    hm = _modnorm(xm_ref[0], g, scale, shift)
    hp = _modnorm(xp_ref[0], g, scale, shift) * jnp.where(first, 0.0, 1.0)
    hn = _modnorm(xn_ref[0], g, scale, shift) * jnp.where(last, 0.0, 1.0)
    return jnp.concatenate([hp, hm, hn], axis=0).astype(BF16)


def _dwconv3(p, w_ref, b_ref, rows):
    n = p.shape[0]
    prev = pltpu.roll(p, 1, 0)[HALO:HALO + rows]
    nxt = pltpu.roll(p, n - 1, 0)[HALO:HALO + rows]
    cur = p[HALO:HALO + rows]
    return b_ref[...] + prev * w_ref[0:1, :] + cur * w_ref[1:2, :] + nxt * w_ref[2:3, :]


def _mod_kernel(c_ref, w_ref, b_ref, o_ref):
    c = c_ref[...]
    act = c * jax.nn.sigmoid(c)
    o_ref[0] = jnp.dot(act, w_ref[0], preferred_element_type=F32,
                       precision=lax.Precision.HIGHEST) + b_ref[0]


def _modulation(c, ada_w, ada_b):
    depth, d, nm = ada_w.shape
    b = c.shape[0]
    tn = nm // 4
    return pl.pallas_call(
        _mod_kernel,
        grid=(depth, nm // tn),
        in_specs=[pl.BlockSpec((b, d), lambda l, j: (0, 0)),
                  pl.BlockSpec((1, d, tn), lambda l, j: (l, 0, j)),
                  pl.BlockSpec((1, 1, tn), lambda l, j: (l, 0, j))],
        out_specs=pl.BlockSpec((1, b, tn), lambda l, j: (l, 0, j)),
        out_shape=jax.ShapeDtypeStruct((depth, b, nm), F32),
        compiler_params=_params(("parallel", "parallel")),
        name="adaln_modulation",
    )(c, ada_w, ada_b.reshape(depth, 1, nm))


def _stage1(load_x, t_ref, s1_ref, n_iter, rows_out, pitch):
    def body(n2, carry):
        t = t_ref[pl.ds(pl.multiple_of(n2 * rows_out, rows_out), rows_out), :]
        a = jnp.dot(t, load_x(n2), preferred_element_type=F32)
        s1_ref[pl.ds(pl.multiple_of(n2 * pitch, SUBLANES), rows_out), :] = a
        return carry
    lax.fori_loop(0, n_iter, body, 0)


def _load_slab_column(s1_ref, k1, n1n, n2n, pitch):
    ar = s1_ref[pl.ds(k1, n2n, stride=pitch), :]
    ai = s1_ref[pl.ds(n1n + k1, n2n, stride=pitch), :]
    return jnp.concatenate([ar, ai], axis=0).astype(BF16)


def _filter_kernel(zf_ref, w1_ref, b1_ref, w2_ref, b2_ref, fr_ref, w3f_ref, w3b_ref, dl_ref,
                   tf_ref, m2_ref, kf_ref, fwd_ref, bwd_ref, s1_ref, *, L):
    n2n = CONV_N2
    nh = L // n2n
    n1n = 2 * nh
    pitch = 2 * n1n + SLAB_PAD
    hp = lax.Precision.HIGHEST
    fr = fr_ref[0]
    h = jnp.sin(fr * (jnp.dot(zf_ref[...], w1_ref[0], preferred_element_type=F32, precision=hp)
                      + b1_ref[0]))
    h = jnp.sin(fr * (jnp.dot(h, w2_ref[0], preferred_element_type=F32, precision=hp)
                      + b2_ref[0]))
    hf = jnp.dot(h, w3f_ref[0], preferred_element_type=F32, precision=hp)
    hb = jnp.dot(h, w3b_ref[0], preferred_element_type=F32, precision=hp)
    pos = lax.broadcasted_iota(jnp.int32, (L, LANES), 0)
    t = pos.astype(F32) / float(max(L - 1, 1))
    decay = jnp.exp(-t * dl_ref[...])
    hf = hf * decay
    hb = jnp.where(pos == 0, 0.0, hb * decay)
    norm = (jnp.sum(jnp.abs(hf), axis=0, keepdims=True)
            + jnp.sum(jnp.abs(hb), axis=0, keepdims=True) + FILTER_EPS)
    inv = 1.0 / (norm * float(2 * L))
    fwd_ref[...] = hf * inv
    bwd_ref[...] = hb * inv

    zeros = jnp.zeros((nh, LANES), F32)
    for src_ref, conj in ((fwd_ref, False), (bwd_ref, True)):
        def load_x(n2, src_ref=src_ref):
            xr = src_ref[pl.ds(n2, nh, stride=n2n), :]
            return jnp.concatenate([xr, zeros], axis=0).astype(BF16)
        _stage1(load_x, tf_ref, s1_ref, n2n, 2 * n1n, pitch)

        def body(k1, carry, conj=conj):
            a = _load_slab_column(s1_ref, k1, n1n, n2n, pitch)
            u = jnp.dot(m2_ref[...], a, preferred_element_type=F32)
            base = pl.multiple_of(k1 * 2 * n2n, 2 * n2n)
            if not conj:
                kf_ref[0, 0, pl.ds(base, 2 * n2n), :] = u
            else:
                kf_ref[0, 0, pl.ds(base, n2n), :] += u[:n2n]
                kf_ref[0, 0, pl.ds(base + n2n, n2n), :] -= u[n2n:]
            return carry
        lax.fori_loop(0, n1n, body, 0)


def _filter_spectra(L, w1p, b1, w2, b2, freq, w3, dh):
    depth = w1p.shape[0]
    n2n = CONV_N2
    nh = L // n2n
    n1n = 2 * nh
    pitch = 2 * n1n + SLAB_PAD
    tf, _, m2, _ = _conv_tables(L)
    zf = _filter_features(L)
    ct = dh // LANES
    fh = FILTER_HIDDEN
    cols = N_DIRS * ct

    def l_of(i):
        return i // (HYENA_ORDER * ct)

    def o_of(i):
        return (i // ct) % HYENA_ORDER

    return pl.pallas_call(
        functools.partial(_filter_kernel, L=L),
        grid=(depth * HYENA_ORDER * ct,),
        in_specs=[_const_spec(zf.shape),
                  pl.BlockSpec((1, LANES, fh), lambda i: (l_of(i), 0, 0)),
                  pl.BlockSpec((1, 1, fh), lambda i: (l_of(i), 0, 0)),
                  pl.BlockSpec((1, fh, fh), lambda i: (l_of(i), 0, 0)),
                  pl.BlockSpec((1, 1, fh), lambda i: (l_of(i), 0, 0)),
                  pl.BlockSpec((1, 1, fh), lambda i: (l_of(i), 0, 0)),
                  pl.BlockSpec((1, fh, LANES), lambda i: (l_of(i), 0, o_of(i) * cols + i % ct)),
                  pl.BlockSpec((1, fh, LANES),
                               lambda i: (l_of(i), 0, o_of(i) * cols + ct + i % ct)),
                  pl.BlockSpec((1, LANES), lambda i: (0, i % ct)),
                  _const_spec(tf.shape), _const_spec(m2.shape)],
        out_specs=pl.BlockSpec((1, 1, n1n * 2 * n2n, LANES),
                               lambda i: (l_of(i), o_of(i), 0, i % ct)),
        out_shape=jax.ShapeDtypeStruct((depth, HYENA_ORDER, n1n * 2 * n2n, dh), F32),
        scratch_shapes=[pltpu.VMEM((L, LANES), F32), pltpu.VMEM((L, LANES), F32),
                        pltpu.VMEM((n2n * pitch, LANES), F32)],
        compiler_params=_params(("parallel",)),
        name="hyena_filter_spectrum",
    )(zf, w1p, b1, w2, b2, freq, w3, w3, _decay_rates(dh), tf, m2)


def _conv_kernel(z_ref, g_ref, kf_ref, d_ref, tf_ref, ti_ref, m2_ref, m2i_ref, o_ref, s1_ref,
                 *, L):
    n2n = CONV_N2
    nh = L // n2n
    n1n = 2 * nh
    pitch = 2 * n1n + SLAB_PAD

    def load_x(n2):
        zr = z_ref[0, 0, pl.ds(n2, nh, stride=n2n), :]
        zi = z_ref[1, 0, pl.ds(n2, nh, stride=n2n), :]
        return jnp.concatenate([zr, zi], axis=0).astype(BF16)
    _stage1(load_x, tf_ref, s1_ref, n2n, 2 * n1n, pitch)

    def mid(k1, carry):
        a = _load_slab_column(s1_ref, k1, n1n, n2n, pitch)
        u = jnp.dot(m2_ref[...], a, preferred_element_type=F32)
        ur, ui = u[:n2n], u[n2n:]
        base = pl.multiple_of(k1 * 2 * n2n, 2 * n2n)
        kr = kf_ref[0, 0, pl.ds(base, n2n), :]
        ki = kf_ref[0, 0, pl.ds(base + n2n, n2n), :]
        v = jnp.concatenate([ur * kr - ui * ki, ur * ki + ui * kr], axis=0).astype(BF16)
        w = jnp.dot(m2i_ref[...], v, preferred_element_type=F32)
        s1_ref[pl.ds(k1, n2n, stride=pitch), :] = w[:n2n]
        s1_ref[pl.ds(n1n + k1, n2n, stride=pitch), :] = w[n2n:]
        return carry
    lax.fori_loop(0, n1n, mid, 0)

    d = d_ref[0, 0]

    def last(n2, carry):
        w = s1_ref[pl.ds(pl.multiple_of(n2 * pitch, SUBLANES), 2 * n1n), :].astype(BF16)
        t = ti_ref[pl.ds(pl.multiple_of(n2 * 2 * nh, 2 * nh), 2 * nh), :]
        y = jnp.dot(t, w, preferred_element_type=F32)
        for half in range(2):
            zz = z_ref[half, 0, pl.ds(n2, nh, stride=n2n), :]
            gg = g_ref[half, 0, pl.ds(n2, nh, stride=n2n), :]
            o_ref[half, 0, pl.ds(n2, nh, stride=n2n), :] = gg * (y[half * nh:(half + 1) * nh]
                                                                + d * zz)
        return carry
    lax.fori_loop(0, n2n, last, 0)


def _long_conv(z, z_col0, gate, gate_col0, kf, d, layer, order):
    _, bh, L, _ = z.shape
    dh = kf.shape[-1]
    ct = dh // LANES
    n2n = CONV_N2
    nh = L // n2n
    n1n = 2 * nh
    pitch = 2 * n1n + SLAB_PAD
    tf, ti, m2, m2i = _conv_tables(L)
    blk = (2, 1, L, LANES)
    return pl.pallas_call(
        functools.partial(_conv_kernel, L=L),
        grid=(ct, bh),
        in_specs=[pl.BlockSpec(blk, lambda c, p: (0, p, 0, z_col0 + c)),
                  pl.BlockSpec(blk, lambda c, p: (0, p, 0, gate_col0 + c)),
                  pl.BlockSpec((1, 1, n1n * 2 * n2n, LANES), lambda c, p: (layer, order, 0, c),
                               pipeline_mode=pl.Buffered(1)),
                  pl.BlockSpec((1, 1, 1, LANES), lambda c, p: (layer, order, 0, c)),
                  _const_spec(tf.shape), _const_spec(ti.shape),
                  _const_spec(m2.shape), _const_spec(m2i.shape)],
        out_specs=pl.BlockSpec(blk, lambda c, p: (0, p, 0, c)),
        out_shape=jax.ShapeDtypeStruct((2, bh, L, dh), F32),
        scratch_shapes=[pltpu.VMEM((n2n * pitch, LANES), F32)],
        compiler_params=_params(("parallel", "parallel")),
        name="hyena_long_conv",
    )(z, gate, kf, d, tf, ti, m2, m2i)


def _fnet_kernel(p_ref, cs_ref, tf_ref, m4_ref, wg_ref, o_ref, a_ref, b_ref, s1_ref, *, L):
    nb = FNET_NB
    na = L // nb
    pitch = 2 * na + SLAB_PAD
    ab = jnp.dot(p_ref[0].astype(BF16), cs_ref[...], preferred_element_type=F32)
    a_ref[...] = ab[:, :LANES]
    b_ref[...] = ab[:, LANES:]

    def load_x(n2):
        zr = a_ref[pl.ds(n2, na, stride=nb), :]
        zi = b_ref[pl.ds(n2, na, stride=nb), :]
        return jnp.concatenate([zr, zi], axis=0).astype(BF16)
    _stage1(load_x, tf_ref, s1_ref, nb, 2 * na, pitch)

    def body(k1, carry):
        a = _load_slab_column(s1_ref, k1, na, nb, pitch)
        r = jnp.dot(m4_ref[...], a, preferred_element_type=F32)
        y = jnp.dot(r.astype(BF16), wg_ref[0, 0], preferred_element_type=F32)
        o_ref[0, pl.ds(k1, nb, stride=na), :] = y
        return carry
    lax.fori_loop(0, na, body, 0)


def _fnet_mixer(pf, wg, layer):
    b, L, df = pf.shape
    cg = df // FNET_GROUPS
    cs, tf, m4 = _fnet_tables(L, cg)
    nb = FNET_NB
    na = L // nb
    pitch = 2 * na + SLAB_PAD
    ct = df // LANES
    return pl.pallas_call(
        functools.partial(_fnet_kernel, L=L),
        grid=(b, ct),
        in_specs=[pl.BlockSpec((1, L, LANES), lambda i, c: (i, 0, c)),
                  _const_spec(cs.shape), _const_spec(tf.shape), _const_spec(m4.shape),
                  pl.BlockSpec((1, 1, LANES, LANES), lambda i, c: (layer, c, 0, 0))],
        out_specs=pl.BlockSpec((1, L, LANES), lambda i, c: (i, 0, c)),
        out_shape=jax.ShapeDtypeStruct((b, L, df), F32),
        scratch_shapes=[pltpu.VMEM((L, LANES), F32), pltpu.VMEM((L, LANES), F32),
                        pltpu.VMEM((nb * pitch, LANES), F32)],
        compiler_params=_params(("parallel", "parallel")),
        name="fnet_mixer",
    )(pf, cs, tf, m4, wg)


def _inproj_kernel(xm_ref, xp_ref, xn_ref, mod_ref, g_ref, w_ref, sw_ref, sb_ref,
                   u_ref, pf_ref, *, tm, dconv):
    i = pl.program_id(1)
    shift, scale = mod_ref[0, 0, 0:1, :], mod_ref[0, 0, 1:2, :]
    h = _halo_rows(xm_ref, xp_ref, xn_ref, g_ref[0], scale, shift,
                   i == 0, i == pl.num_programs(1) - 1)
    p = jnp.dot(h, w_ref[0], preferred_element_type=F32)
    u_ref[0] = _dwconv3(p[:, :dconv], sw_ref.at[0], sb_ref.at[0], tm)
    pf_ref[0] = p[HALO:HALO + tm, dconv:]


def _row_halo_specs(tm, L, d):
    nblk = L // HALO
    step = tm // HALO
    return [pl.BlockSpec((1, tm, d), lambda b, i: (b, i, 0)),
            pl.BlockSpec((1, HALO, d), lambda b, i: (b, jnp.maximum(i * step - 1, 0), 0)),
            pl.BlockSpec((1, HALO, d), lambda b, i: (b, jnp.minimum((i + 1) * step, nblk - 1), 0))]


def _in_projection(x, mod, g_pre, w_in, short_w, short_b, layer, tm):
    b, L, d = x.shape
    e = w_in.shape[-1]
    dconv = short_w.shape[-1]
    return pl.pallas_call(
        functools.partial(_inproj_kernel, tm=tm, dconv=dconv),
        grid=(b, L // tm),
        in_specs=_row_halo_specs(tm, L, d) + [
            pl.BlockSpec((1, 1, N_MOD, d), lambda b_, i: (layer, b_, 0, 0)),
            pl.BlockSpec((1, 1, d), lambda b_, i: (layer, 0, 0)),
            pl.BlockSpec((1, d, e), lambda b_, i: (layer, 0, 0)),
            pl.BlockSpec((1, 3, dconv), lambda b_, i: (layer, 0, 0)),
            pl.BlockSpec((1, 1, dconv), lambda b_, i: (layer, 0, 0))],
        out_specs=[pl.BlockSpec((1, tm, dconv), lambda b_, i: (b_, i, 0)),
                   pl.BlockSpec((1, tm, e - dconv), lambda b_, i: (b_, i, 0))],
        out_shape=[jax.ShapeDtypeStruct((b, L, dconv), F32),
                   jax.ShapeDtypeStruct((b, L, e - dconv), F32)],
        compiler_params=_params(("parallel", "parallel")),
        name="in_projection",
    )(x, x, x, mod, g_pre, w_in, short_w, short_b)


def _outproj_kernel(x_ref, yh_ref, yf_ref, mod_ref, g_ref, wh_ref, wf_ref, o_ref):
    y = (jnp.dot(yh_ref[0].astype(BF16), wh_ref[0], preferred_element_type=F32)
         + jnp.dot(yf_ref[0].astype(BF16), wf_ref[0], preferred_element_type=F32))
    gate = mod_ref[0, 0, 2:3, :]
    o_ref[0] = x_ref[0] + gate * _rmsnorm(y, g_ref[0])


def _out_projection(x, yh, yf, mod, g_post, w_out, layer, tm):
    b, L, d = x.shape
    dh = yh.shape[-1]
    df = yf.shape[-1]
    return pl.pallas_call(
        _outproj_kernel,
        grid=(b, L // tm),
        in_specs=[pl.BlockSpec((1, tm, d), lambda b_, i: (b_, i, 0)),
                  pl.BlockSpec((1, tm, dh), lambda b_, i: (b_, i, 0)),
                  pl.BlockSpec((1, tm, df), lambda b_, i: (b_, i, 0)),
                  pl.BlockSpec((1, 1, N_MOD, d), lambda b_, i: (layer, b_, 0, 0)),
                  pl.BlockSpec((1, 1, d), lambda b_, i: (layer, 0, 0)),
                  pl.BlockSpec((1, dh, d), lambda b_, i: (layer, 0, 0)),
                  pl.BlockSpec((1, df, d), lambda b_, i: (layer, dh // df, 0))],
        out_specs=pl.BlockSpec((1, tm, d), lambda b_, i: (b_, i, 0)),
        out_shape=jax.ShapeDtypeStruct((b, L, d), F32),
        compiler_params=_params(("parallel", "parallel")),
        name="out_projection",
    )(x, yh, yf, mod, g_post, w_out, w_out)


def _ffn_kernel(xm_ref, xp_ref, xn_ref, mod_ref, gpre_ref, gpost_ref, wa_ref, wb_ref,
                cwa_ref, cwb_ref, cba_ref, cbb_ref, wd_ref, o_ref, h_ref, acc_ref, *, tm):
    i = pl.program_id(1)
    f = pl.program_id(2)

    @pl.when(f == 0)
    def _():
        shift, scale = mod_ref[0, 0, 3:4, :], mod_ref[0, 0, 4:5, :]
        h_ref[...] = _halo_rows(xm_ref, xp_ref, xn_ref, gpre_ref[0], scale, shift,
                                i == 0, i == pl.num_programs(1) - 1)
        acc_ref[...] = jnp.zeros_like(acc_ref)

    h = h_ref[...]
    a = _dwconv3(jnp.dot(h, wa_ref[0], preferred_element_type=F32),
                 cwa_ref.at[0], cba_ref.at[0], tm)
    b = _dwconv3(jnp.dot(h, wb_ref[0], preferred_element_type=F32),
                 cwb_ref.at[0], cbb_ref.at[0], tm)
    act = (0.5 * a * (1.0 + lax.erf(a * (1.0 / math.sqrt(2.0)))) * b).astype(BF16)
    acc_ref[...] += jnp.dot(act, wd_ref[0], preferred_element_type=F32)

    @pl.when(f == pl.num_programs(2) - 1)
    def _():
        gate = mod_ref[0, 0, 5:6, :]
        o_ref[0] = xm_ref[0] + gate * _rmsnorm(acc_ref[...], gpost_ref[0])


def _ffn(x, mod, g_pre, g_post, w_up, dw_w, dw_b, w_down, layer, tm, tf):
    b, L, d = x.shape
    dff = w_down.shape[1]
    nf = dff // tf
    idx = lambda b_, i, f: (layer, 0, 0)
    return pl.pallas_call(
        functools.partial(_ffn_kernel, tm=tm),
        grid=(b, L // tm, nf),
        in_specs=[pl.BlockSpec((1, tm, d), lambda b_, i, f: (b_, i, 0)),
                  pl.BlockSpec((1, HALO, d),
                               lambda b_, i, f: (b_, jnp.maximum(i * (tm // HALO) - 1, 0), 0)),
                  pl.BlockSpec((1, HALO, d),
                               lambda b_, i, f: (b_, jnp.minimum((i + 1) * (tm // HALO),
                                                                 L // HALO - 1), 0)),
                  pl.BlockSpec((1, 1, N_MOD, d), lambda b_, i, f: (layer, b_, 0, 0)),
                  pl.BlockSpec((1, 1, d), idx), pl.BlockSpec((1, 1, d), idx),
                  pl.BlockSpec((1, d, tf), lambda b_, i, f: (layer, 0, f)),
                  pl.BlockSpec((1, d, tf), lambda b_, i, f: (layer, 0, nf + f)),
                  pl.BlockSpec((1, 3, tf), lambda b_, i, f: (layer, 0, f)),
                  pl.BlockSpec((1, 3, tf), lambda b_, i, f: (layer, 0, nf + f)),
                  pl.BlockSpec((1, 1, tf), lambda b_, i, f: (layer, 0, f)),
                  pl.BlockSpec((1, 1, tf), lambda b_, i, f: (layer, 0, nf + f)),
                  pl.BlockSpec((1, tf, d), lambda b_, i, f: (layer, f, 0))],
        out_specs=pl.BlockSpec((1, tm, d), lambda b_, i, f: (b_, i, 0)),
        out_shape=jax.ShapeDtypeStruct((b, L, d), F32),
        scratch_shapes=[pltpu.VMEM((tm + 2 * HALO, d), BF16), pltpu.VMEM((tm, d), F32)],
        compiler_params=_params(("parallel", "parallel", "arbitrary")),
        name="geglu_ffn",
    )(x, x, x, mod, g_pre, g_post, w_up, w_up, dw_w, dw_w, dw_b, dw_b, w_down)


def kernel(x, c, ada_w, ada_b, g_mix_pre, g_mix_post, w_in, short_w, short_b, filt_w1, filt_b1,
           filt_w2, filt_b2, filt_freq, filt_w3, hyena_d, fnet_w, w_out, g_ffn_pre, g_ffn_post,
           w_up, dw_w, dw_b, w_down):
    b, L, d = x.shape
    depth = ada_w.shape[0]
    dh = hyena_d.shape[-1]
    df = w_in.shape[-1] - (HYENA_ORDER + 1) * dh
    cg = df // FNET_GROUPS
    assert b % 2 == 0 and L % (CONV_N2 * SUBLANES) == 0 and L % (FNET_NB * SUBLANES) == 0
    assert dh % LANES == 0 and df % LANES == 0 and LANES % cg == 0
    tm_proj = min(512, L)
    tm_ffn = min(1024, L)
    tf = 256

    row = lambda a: a.reshape(depth, 1, a.shape[-1])
    mod = _modulation(c, ada_w, ada_b).reshape(depth, b, N_MOD, d)

    w1p = jnp.pad(filt_w1, ((0, 0), (0, LANES - filt_w1.shape[1]), (0, 0)))
    kf = _filter_spectra(L, w1p, row(filt_b1), filt_w2, row(filt_b2), row(filt_freq),
                         filt_w3, dh)
    hd = hyena_d.reshape(depth, HYENA_ORDER, 1, dh)

    gpt = LANES // cg
    eye = jnp.eye(gpt, dtype=F32)
    wg = jnp.einsum("ltacd,ae->ltaced", fnet_w.reshape(depth, df // LANES, gpt, cg, cg), eye)
    wg = wg.reshape(depth, df // LANES, LANES, LANES).astype(BF16)

    w_in_b = w_in.astype(BF16)
    w_out_b = w_out.astype(BF16)
    w_up_b = w_up.astype(BF16)
    w_down_b = w_down.astype(BF16)
    g_mix_pre, g_mix_post = row(g_mix_pre), row(g_mix_post)
    g_ffn_pre, g_ffn_post = row(g_ffn_pre), row(g_ffn_post)
    short_b3, dw_b3 = row(short_b), row(dw_b)
    nct = dh // LANES

    for l in range(depth):
        u, pf = _in_projection(x, mod, g_mix_pre, w_in_b, short_w, short_b3, l, tm_proj)
        u4 = u.reshape(2, b // 2, L, u.shape[-1])
        z2 = _long_conv(u4, 0, u4, nct, kf, hd, l, 0)
        yh = _long_conv(z2, 0, u4, 2 * nct, kf, hd, l, 1).reshape(b, L, dh)
        yf = _fnet_mixer(pf, wg, l)
        x = _out_projection(x, yh, yf, mod, g_mix_post, w_out_b, l, tm_proj)
        x = _ffn(x, mod, g_ffn_pre, g_ffn_post, w_up_b, dw_w, dw_b3, w_down_b, l, tm_ffn, tf)
    return x
```

```python
import functools
import math

import numpy as np
import jax
import jax.numpy as jnp
from jax import lax
from jax.experimental import pallas as pl
from jax.experimental.pallas import tpu as pltpu

F32 = jnp.float32
BF16 = jnp.bfloat16

LANES = 128
SUBLANES = 8
HALO = SUBLANES
VMEM_LIMIT = 56 * 1024 * 1024

FNET_GROUPS = 8
HYENA_ORDER = 2
N_DIRS = 2
FILTER_BANDS = 16
FILTER_HIDDEN = 64
DECAY_TARGET = 1e-2
FAST_DECAY_PCT = 0.3
SLOW_DECAY_PCT = 1.5
N_MOD = 6
NORM_EPS = 1e-6
FILTER_EPS = 1e-6

CONV_N2 = 128
FNET_NB = 64
SLAB_PAD = 8
UNROLL = 8


def _params(sem, vmem=VMEM_LIMIT):
    return pltpu.CompilerParams(dimension_semantics=sem, vmem_limit_bytes=vmem)


def _const_spec(shape):
    nd = len(shape)
    return pl.BlockSpec(shape, lambda *_: (0,) * nd, pipeline_mode=pl.Buffered(1))


@functools.lru_cache(maxsize=None)
def _conv_tables(L):
    n2n = CONV_N2
    nh = L // n2n
    n1n = 2 * nh
    N = 2 * L
    k1 = np.arange(n1n)[:, None]
    n1 = np.arange(nh)[None, :]
    tf = np.zeros((n2n, 2 * n1n, 2 * nh), np.float64)
    for n2 in range(n2n):
        th = 2.0 * np.pi * ((k1 * (n2n * n1 + n2)) % N) / N
        c, s = np.cos(th), np.sin(th)
        tf[n2] = np.block([[c, s], [-s, c]])
    ti = np.transpose(tf, (0, 2, 1))
    k2 = np.arange(n2n)[:, None]
    m = np.arange(n2n)[None, :]
    th2 = 2.0 * np.pi * ((k2 * m) % n2n) / n2n
    c2, s2 = np.cos(th2), np.sin(th2)
    m2 = np.block([[c2, s2], [-s2, c2]])
    return (jnp.asarray(tf.reshape(n2n * 2 * n1n, 2 * nh), BF16),
            jnp.asarray(ti.reshape(n2n * 2 * nh, 2 * n1n), BF16),
            jnp.asarray(m2, BF16), jnp.asarray(m2.T, BF16))


@functools.lru_cache(maxsize=None)
def _fnet_tables(L, cg):
    nb = FNET_NB
    na = L // nb
    scale = 1.0 / math.sqrt(L * cg)
    j = np.arange(cg)
    thc = 2.0 * np.pi * ((j[:, None] * j[None, :]) % cg) / cg
    reps = LANES // cg
    cc = np.kron(np.eye(reps), np.cos(thc)) * scale
    sc = np.kron(np.eye(reps), np.sin(thc)) * scale
    cs = np.concatenate([cc, -sc], axis=1)
    k1 = np.arange(na)[:, None]
    n1 = np.arange(na)[None, :]
    tf = np.zeros((nb, 2 * na, 2 * na), np.float64)
    for n2 in range(nb):
        th = 2.0 * np.pi * ((k1 * (nb * n1 + n2)) % L) / L
        c, s = np.cos(th), np.sin(th)
        tf[n2] = np.block([[c, s], [-s, c]])
    k2 = np.arange(nb)[:, None]
    m = np.arange(nb)[None, :]
    th2 = 2.0 * np.pi * ((k2 * m) % nb) / nb
    m4 = np.concatenate([np.cos(th2), np.sin(th2)], axis=1)
    return (jnp.asarray(cs, BF16), jnp.asarray(tf.reshape(nb * 2 * na, 2 * na), BF16),
            jnp.asarray(m4, BF16))


@functools.lru_cache(maxsize=None)
def _filter_features(L):
    pos = np.arange(L, dtype=np.float32)
    bands = np.linspace(1e-4, FILTER_BANDS - 1, FILTER_BANDS, dtype=np.float32)
    t = pos / np.float32(max(L - 1, 1))
    ang = np.float32(2.0 * math.pi / L) * pos[:, None] * bands[None, :]
    z = np.concatenate([t[:, None], np.cos(ang), -np.sin(ang)], axis=-1).astype(np.float32)
    zp = np.zeros((L, LANES), np.float32)
    zp[:, :z.shape[1]] = z
    return jnp.asarray(zp)


def _decay_rates(dh):
    min_decay = math.log(DECAY_TARGET) / SLOW_DECAY_PCT
    max_decay = math.log(DECAY_TARGET) / FAST_DECAY_PCT
    return jnp.abs(jnp.linspace(min_decay, max_decay, dh, dtype=F32)).reshape(1, dh)


def _modnorm(x, g, scale, shift):
    ms = jnp.mean(x * x, axis=-1, keepdims=True)
    return (x * lax.rsqrt(ms + NORM_EPS) * g) * (1.0 + scale) + shift


def _rmsnorm(y, g):
    ms = jnp.mean(y * y, axis=-1, keepdims=True)
    return y * lax.rsqrt(ms + NORM_EPS) * g


def _halo_rows(xm_ref, xp_ref, xn_ref, g, scale, shift, first, last):
    hm = _modnorm(xm_ref[0], g, scale, shift)
    hp = _modnorm(xp_ref[0], g, scale, shift) * jnp.where(first, 0.0, 1.0)
    hn = _modnorm(xn_ref[0], g, scale, shift) * jnp.where(last, 0.0, 1.0)
    return jnp.concatenate([hp, hm, hn], axis=0).astype(BF16)


def _dwconv3(p, w_ref, b_ref, rows):
    n = p.shape[0]
    prev = pltpu.roll(p, 1, 0)[HALO:HALO + rows]
    nxt = pltpu.roll(p, n - 1, 0)[HALO:HALO + rows]
    cur = p[HALO:HALO + rows]
    return b_ref[...] + prev * w_ref[0:1, :] + cur * w_ref[1:2, :] + nxt * w_ref[2:3, :]


def _mod_kernel(c_ref, w_ref, b_ref, o_ref):
    c = c_ref[...]
    act = c * jax.nn.sigmoid(c)
    o_ref[0] = jnp.dot(act, w_ref[0], preferred_element_type=F32,
                       precision=lax.Precision.HIGHEST) + b_ref[0]


def _modulation(c, ada_w, ada_b):
    depth, d, nm = ada_w.shape
    b = c.shape[0]
    tn = nm // 4
    return pl.pallas_call(
        _mod_kernel,
        grid=(depth, nm // tn),
        in_specs=[pl.BlockSpec((b, d), lambda l, j: (0, 0)),
                  pl.BlockSpec((1, d, tn), lambda l, j: (l, 0, j)),
                  pl.BlockSpec((1, 1, tn), lambda l, j: (l, 0, j))],
        out_specs=pl.BlockSpec((1, b, tn), lambda l, j: (l, 0, j)),
        out_shape=jax.ShapeDtypeStruct((depth, b, nm), F32),
        compiler_params=_params(("parallel", "parallel")),
        name="adaln_modulation",
    )(c, ada_w, ada_b.reshape(depth, 1, nm))


def _stage1(load_x, t_ref, s1_ref, n_iter, rows_out, pitch):
    def body(n2, carry):
        t = t_ref[pl.ds(pl.multiple_of(n2 * rows_out, rows_out), rows_out), :]
        a = jnp.dot(t, load_x(n2), preferred_element_type=F32)
        s1_ref[pl.ds(pl.multiple_of(n2 * pitch, SUBLANES), rows_out), :] = a
        return carry
    lax.fori_loop(0, n_iter, body, 0, unroll=UNROLL)


def _load_slab_column(s1_ref, k1, n1n, n2n, pitch):
    ar = s1_ref[pl.ds(k1, n2n, stride=pitch), :]
    ai = s1_ref[pl.ds(n1n + k1, n2n, stride=pitch), :]
    return jnp.concatenate([ar, ai], axis=0).astype(BF16)


def _filter_kernel(zf_ref, w1_ref, b1_ref, w2_ref, b2_ref, fr_ref, w3f_ref, w3b_ref, dl_ref,
                   tf_ref, m2_ref, kf_ref, fwd_ref, bwd_ref, s1_ref, *, L):
    n2n = CONV_N2
    nh = L // n2n
    n1n = 2 * nh
    pitch = 2 * n1n + SLAB_PAD
    hp = lax.Precision.HIGHEST
    fr = fr_ref[0]
    h = jnp.sin(fr * (jnp.dot(zf_ref[...], w1_ref[0], preferred_element_type=F32, precision=hp)
                      + b1_ref[0]))
    h = jnp.sin(fr * (jnp.dot(h, w2_ref[0], preferred_element_type=F32, precision=hp)
                      + b2_ref[0]))
    hf = jnp.dot(h, w3f_ref[0], preferred_element_type=F32, precision=hp)
    hb = jnp.dot(h, w3b_ref[0], preferred_element_type=F32, precision=hp)
    pos = lax.broadcasted_iota(jnp.int32, (L, LANES), 0)
    t = pos.astype(F32) / float(max(L - 1, 1))
    decay = jnp.exp(-t * dl_ref[...])
    hf = hf * decay
    hb = jnp.where(pos == 0, 0.0, hb * decay)
    norm = (jnp.sum(jnp.abs(hf), axis=0, keepdims=True)
            + jnp.sum(jnp.abs(hb), axis=0, keepdims=True) + FILTER_EPS)
    inv = 1.0 / (norm * float(2 * L))
    fwd_ref[...] = hf * inv
    bwd_ref[...] = hb * inv

    zeros = jnp.zeros((nh, LANES), F32)
    for src_ref, conj in ((fwd_ref, False), (bwd_ref, True)):
        def load_x(n2, src_ref=src_ref):
            xr = src_ref[pl.ds(n2, nh, stride=n2n), :]
            return jnp.concatenate([xr, zeros], axis=0).astype(BF16)
        _stage1(load_x, tf_ref, s1_ref, n2n, 2 * n1n, pitch)

        def body(k1, carry, conj=conj):
            a = _load_slab_column(s1_ref, k1, n1n, n2n, pitch)
            u = jnp.dot(m2_ref[...], a, preferred_element_type=F32)
            base = pl.multiple_of(k1 * 2 * n2n, 2 * n2n)
            if not conj:
                kf_ref[0, 0, pl.ds(base, 2 * n2n), :] = u
            else:
                kf_ref[0, 0, pl.ds(base, n2n), :] += u[:n2n]
                kf_ref[0, 0, pl.ds(base + n2n, n2n), :] -= u[n2n:]
            return carry
        lax.fori_loop(0, n1n, body, 0, unroll=UNROLL)


def _filter_spectra(L, w1p, b1, w2, b2, freq, w3, dh):
    depth = w1p.shape[0]
    n2n = CONV_N2
    nh = L // n2n
    n1n = 2 * nh
    pitch = 2 * n1n + SLAB_PAD
    tf, _, m2, _ = _conv_tables(L)
    zf = _filter_features(L)
    ct = dh // LANES
    fh = FILTER_HIDDEN
    cols = N_DIRS * ct

    def l_of(i):
        return i // (HYENA_ORDER * ct)

    def o_of(i):
        return (i // ct) % HYENA_ORDER

    return pl.pallas_call(
        functools.partial(_filter_kernel, L=L),
        grid=(depth * HYENA_ORDER * ct,),
        in_specs=[_const_spec(zf.shape),
                  pl.BlockSpec((1, LANES, fh), lambda i: (l_of(i), 0, 0)),
                  pl.BlockSpec((1, 1, fh), lambda i: (l_of(i), 0, 0)),
                  pl.BlockSpec((1, fh, fh), lambda i: (l_of(i), 0, 0)),
                  pl.BlockSpec((1, 1, fh), lambda i: (l_of(i), 0, 0)),
                  pl.BlockSpec((1, 1, fh), lambda i: (l_of(i), 0, 0)),
                  pl.BlockSpec((1, fh, LANES), lambda i: (l_of(i), 0, o_of(i) * cols + i % ct)),
                  pl.BlockSpec((1, fh, LANES),
                               lambda i: (l_of(i), 0, o_of(i) * cols + ct + i % ct)),
                  pl.BlockSpec((1, LANES), lambda i: (0, i % ct)),
                  _const_spec(tf.shape), _const_spec(m2.shape)],
        out_specs=pl.BlockSpec((1, 1, n1n * 2 * n2n, LANES),
                               lambda i: (l_of(i), o_of(i), 0, i % ct)),
        out_shape=jax.ShapeDtypeStruct((depth, HYENA_ORDER, n1n * 2 * n2n, dh), F32),
        scratch_shapes=[pltpu.VMEM((L, LANES), F32), pltpu.VMEM((L, LANES), F32),
                        pltpu.VMEM((n2n * pitch, LANES), F32)],
        compiler_params=_params(("parallel",)),
        name="hyena_filter_spectrum",
    )(zf, w1p, b1, w2, b2, freq, w3, w3, _decay_rates(dh), tf, m2)


def _conv_kernel(z_ref, g_ref, kf_ref, d_ref, tf_ref, ti_ref, m2_ref, m2i_ref, o_ref, s1_ref,
                 *, L):
    n2n = CONV_N2
    nh = L // n2n
    n1n = 2 * nh
    pitch = 2 * n1n + SLAB_PAD

    def load_x(n2):
        zr = z_ref[0, 0, pl.ds(n2, nh, stride=n2n), :]
        zi = z_ref[1, 0, pl.ds(n2, nh, stride=n2n), :]
        return jnp.concatenate([zr, zi], axis=0).astype(BF16)
    _stage1(load_x, tf_ref, s1_ref, n2n, 2 * n1n, pitch)

    def mid(j, carry):
        cols = [j * UNROLL + i for i in range(UNROLL)]
        a = [_load_slab_column(s1_ref, k1, n1n, n2n, pitch) for k1 in cols]
        w = []
        for k1, ak in zip(cols, a):
            u = jnp.dot(m2_ref[...], ak, preferred_element_type=F32)
            ur, ui = u[:n2n], u[n2n:]
            base = pl.multiple_of(k1 * 2 * n2n, 2 * n2n)
            kr = kf_ref[0, 0, pl.ds(base, n2n), :]
            ki = kf_ref[0, 0, pl.ds(base + n2n, n2n), :]
            v = jnp.concatenate([ur * kr - ui * ki, ur * ki + ui * kr], axis=0).astype(BF16)
            w.append(jnp.dot(m2i_ref[...], v, preferred_element_type=F32))
        for k1, wk in zip(cols, w):
            s1_ref[pl.ds(k1, n2n, stride=pitch), :] = wk[:n2n]
            s1_ref[pl.ds(n1n + k1, n2n, stride=pitch), :] = wk[n2n:]
        return carry
    lax.fori_loop(0, n1n // UNROLL, mid, 0)

    d = d_ref[0, 0]

    def last(n2, carry):
        w = s1_ref[pl.ds(pl.multiple_of(n2 * pitch, SUBLANES), 2 * n1n), :].astype(BF16)
        t = ti_ref[pl.ds(pl.multiple_of(n2 * 2 * nh, 2 * nh), 2 * nh), :]
        y = jnp.dot(t, w, preferred_element_type=F32)
        for half in range(2):
            zz = z_ref[half, 0, pl.ds(n2, nh, stride=n2n), :]
            gg = g_ref[half, 0, pl.ds(n2, nh, stride=n2n), :]
            o_ref[half, 0, pl.ds(n2, nh, stride=n2n), :] = gg * (y[half * nh:(half + 1) * nh]
                                                                + d * zz)
        return carry
    lax.fori_loop(0, n2n, last, 0, unroll=UNROLL)


def _long_conv(z, z_col0, gate, gate_col0, kf, d, layer, order):
    _, bh, L, _ = z.shape
    dh = kf.shape[-1]
    ct = dh // LANES
    n2n = CONV_N2
    nh = L // n2n
    n1n = 2 * nh
    pitch = 2 * n1n + SLAB_PAD
    tf, ti, m2, m2i = _conv_tables(L)
    blk = (2, 1, L, LANES)
    return pl.pallas_call(
        functools.partial(_conv_kernel, L=L),
        grid=(ct, bh),
        in_specs=[pl.BlockSpec(blk, lambda c, p: (0, p, 0, z_col0 + c)),
                  pl.BlockSpec(blk, lambda c, p: (0, p, 0, gate_col0 + c)),
                  pl.BlockSpec((1, 1, n1n * 2 * n2n, LANES), lambda c, p: (layer, order, 0, c),
                               pipeline_mode=pl.Buffered(1)),
                  pl.BlockSpec((1, 1, 1, LANES), lambda c, p: (layer, order, 0, c)),
                  _const_spec(tf.shape), _const_spec(ti.shape),
                  _const_spec(m2.shape), _const_spec(m2i.shape)],
        out_specs=pl.BlockSpec(blk, lambda c, p: (0, p, 0, c)),
        out_shape=jax.ShapeDtypeStruct((2, bh, L, dh), F32),
        scratch_shapes=[pltpu.VMEM((n2n * pitch, LANES), F32)],
        compiler_params=_params(("parallel", "parallel")),
        name="hyena_long_conv",
    )(z, gate, kf, d, tf, ti, m2, m2i)


def _fnet_kernel(p_ref, cs_ref, tf_ref, m4_ref, wg_ref, o_ref, a_ref, b_ref, s1_ref, *, L):
    nb = FNET_NB
    na = L // nb
    pitch = 2 * na + SLAB_PAD
    ab = jnp.dot(p_ref[0].astype(BF16), cs_ref[...], preferred_element_type=F32)
    a_ref[...] = ab[:, :LANES]
    b_ref[...] = ab[:, LANES:]

    def load_x(n2):
        zr = a_ref[pl.ds(n2, na, stride=nb), :]
        zi = b_ref[pl.ds(n2, na, stride=nb), :]
        return jnp.concatenate([zr, zi], axis=0).astype(BF16)
    _stage1(load_x, tf_ref, s1_ref, nb, 2 * na, pitch)

    def body(k1, carry):
        a = _load_slab_column(s1_ref, k1, na, nb, pitch)
        r = jnp.dot(m4_ref[...], a, preferred_element_type=F32)
        y = jnp.dot(r.astype(BF16), wg_ref[0, 0], preferred_element_type=F32)
        o_ref[0, pl.ds(k1, nb, stride=na), :] = y
        return carry
    lax.fori_loop(0, na, body, 0, unroll=UNROLL)


def _fnet_mixer(pf, wg, layer):
    b, L, df = pf.shape
    cg = df // FNET_GROUPS
    cs, tf, m4 = _fnet_tables(L, cg)
    nb = FNET_NB
    na = L // nb
    pitch = 2 * na + SLAB_PAD
    ct = df // LANES
    return pl.pallas_call(
        functools.partial(_fnet_kernel, L=L),
        grid=(b, ct),
        in_specs=[pl.BlockSpec((1, L, LANES), lambda i, c: (i, 0, c)),
                  _const_spec(cs.shape), _const_spec(tf.shape), _const_spec(m4.shape),
                  pl.BlockSpec((1, 1, LANES, LANES), lambda i, c: (layer, c, 0, 0))],
        out_specs=pl.BlockSpec((1, L, LANES), lambda i, c: (i, 0, c)),
        out_shape=jax.ShapeDtypeStruct((b, L, df), F32),
        scratch_shapes=[pltpu.VMEM((L, LANES), F32), pltpu.VMEM((L, LANES), F32),
                        pltpu.VMEM((nb * pitch, LANES), F32)],
        compiler_params=_params(("parallel", "parallel")),
        name="fnet_mixer",
    )(pf, cs, tf, m4, wg)


def _inproj_kernel(xm_ref, xp_ref, xn_ref, mod_ref, g_ref, w_ref, sw_ref, sb_ref,
                   u_ref, pf_ref, *, tm, dconv):
    i = pl.program_id(1)
    shift, scale = mod_ref[0, 0, 0:1, :], mod_ref[0, 0, 1:2, :]
    h = _halo_rows(xm_ref, xp_ref, xn_ref, g_ref[0], scale, shift,
                   i == 0, i == pl.num_programs(1) - 1)
    p = jnp.dot(h, w_ref[0], preferred_element_type=F32)
    u_ref[0] = _dwconv3(p[:, :dconv], sw_ref.at[0], sb_ref.at[0], tm)
    pf_ref[0] = p[HALO:HALO + tm, dconv:]


def _row_halo_specs(tm, L, d):
    nblk = L // HALO
    step = tm // HALO
    return [pl.BlockSpec((1, tm, d), lambda b, i: (b, i, 0)),
            pl.BlockSpec((1, HALO, d), lambda b, i: (b, jnp.maximum(i * step - 1, 0), 0)),
            pl.BlockSpec((1, HALO, d), lambda b, i: (b, jnp.minimum((i + 1) * step, nblk - 1), 0))]


def _in_projection(x, mod, g_pre, w_in, short_w, short_b, layer, tm):
    b, L, d = x.shape
    e = w_in.shape[-1]
    dconv = short_w.shape[-1]
    return pl.pallas_call(
        functools.partial(_inproj_kernel, tm=tm, dconv=dconv),
        grid=(b, L // tm),
        in_specs=_row_halo_specs(tm, L, d) + [
            pl.BlockSpec((1, 1, N_MOD, d), lambda b_, i: (layer, b_, 0, 0)),
            pl.BlockSpec((1, 1, d), lambda b_, i: (layer, 0, 0)),
            pl.BlockSpec((1, d, e), lambda b_, i: (layer, 0, 0)),
            pl.BlockSpec((1, 3, dconv), lambda b_, i: (layer, 0, 0)),
            pl.BlockSpec((1, 1, dconv), lambda b_, i: (layer, 0, 0))],
        out_specs=[pl.BlockSpec((1, tm, dconv), lambda b_, i: (b_, i, 0)),
                   pl.BlockSpec((1, tm, e - dconv), lambda b_, i: (b_, i, 0))],
        out_shape=[jax.ShapeDtypeStruct((b, L, dconv), F32),
                   jax.ShapeDtypeStruct((b, L, e - dconv), F32)],
        compiler_params=_params(("parallel", "parallel")),
        name="in_projection",
    )(x, x, x, mod, g_pre, w_in, short_w, short_b)


def _outproj_kernel(x_ref, yh_ref, yf_ref, mod_ref, g_ref, wh_ref, wf_ref, o_ref):
    y = (jnp.dot(yh_ref[0].astype(BF16), wh_ref[0], preferred_element_type=F32)
         + jnp.dot(yf_ref[0].astype(BF16), wf_ref[0], preferred_element_type=F32))
    gate = mod_ref[0, 0, 2:3, :]
    o_ref[0] = x_ref[0] + gate * _rmsnorm(y, g_ref[0])


def _out_projection(x, yh, yf, mod, g_post, w_out, layer, tm):
    b, L, d = x.shape
    dh = yh.shape[-1]
    df = yf.shape[-1]
    return pl.pallas_call(
        _outproj_kernel,
        grid=(b, L // tm),
        in_specs=[pl.BlockSpec((1, tm, d), lambda b_, i: (b_, i, 0)),
                  pl.BlockSpec((1, tm, dh), lambda b_, i: (b_, i, 0)),
                  pl.BlockSpec((1, tm, df), lambda b_, i: (b_, i, 0)),
                  pl.BlockSpec((1, 1, N_MOD, d), lambda b_, i: (layer, b_, 0, 0)),
                  pl.BlockSpec((1, 1, d), lambda b_, i: (layer, 0, 0)),
                  pl.BlockSpec((1, dh, d), lambda b_, i: (layer, 0, 0)),
                  pl.BlockSpec((1, df, d), lambda b_, i: (layer, dh // df, 0))],
        out_specs=pl.BlockSpec((1, tm, d), lambda b_, i: (b_, i, 0)),
        out_shape=jax.ShapeDtypeStruct((b, L, d), F32),
        compiler_params=_params(("parallel", "parallel")),
        name="out_projection",
    )(x, yh, yf, mod, g_post, w_out, w_out)


def _ffn_kernel(xm_ref, xp_ref, xn_ref, mod_ref, gpre_ref, gpost_ref, wa_ref, wb_ref,
                cwa_ref, cwb_ref, cba_ref, cbb_ref, wd_ref, o_ref, h_ref, acc_ref, *, tm):
    i = pl.program_id(1)
    f = pl.program_id(2)

    @pl.when(f == 0)
    def _():
        shift, scale = mod_ref[0, 0, 3:4, :], mod_ref[0, 0, 4:5, :]
        h_ref[...] = _halo_rows(xm_ref, xp_ref, xn_ref, gpre_ref[0], scale, shift,
                                i == 0, i == pl.num_programs(1) - 1)
        acc_ref[...] = jnp.zeros_like(acc_ref)

    h = h_ref[...]
    a = _dwconv3(jnp.dot(h, wa_ref[0], preferred_element_type=F32),
                 cwa_ref.at[0], cba_ref.at[0], tm)
    b = _dwconv3(jnp.dot(h, wb_ref[0], preferred_element_type=F32),
                 cwb_ref.at[0], cbb_ref.at[0], tm)
    act = (0.5 * a * (1.0 + lax.erf(a * (1.0 / math.sqrt(2.0)))) * b).astype(BF16)
    acc_ref[...] += jnp.dot(act, wd_ref[0], preferred_element_type=F32)

    @pl.when(f == pl.num_programs(2) - 1)
    def _():
        gate = mod_ref[0, 0, 5:6, :]
        o_ref[0] = xm_ref[0] + gate * _rmsnorm(acc_ref[...], gpost_ref[0])


def _ffn(x, mod, g_pre, g_post, w_up, dw_w, dw_b, w_down, layer, tm, tf):
    b, L, d = x.shape
    dff = w_down.shape[1]
    nf = dff // tf
    idx = lambda b_, i, f: (layer, 0, 0)
    return pl.pallas_call(
        functools.partial(_ffn_kernel, tm=tm),
        grid=(b, L // tm, nf),
        in_specs=[pl.BlockSpec((1, tm, d), lambda b_, i, f: (b_, i, 0)),
                  pl.BlockSpec((1, HALO, d),
                               lambda b_, i, f: (b_, jnp.maximum(i * (tm // HALO) - 1, 0), 0)),
                  pl.BlockSpec((1, HALO, d),
                               lambda b_, i, f: (b_, jnp.minimum((i + 1) * (tm // HALO),
                                                                 L // HALO - 1), 0)),
                  pl.BlockSpec((1, 1, N_MOD, d), lambda b_, i, f: (layer, b_, 0, 0)),
                  pl.BlockSpec((1, 1, d), idx), pl.BlockSpec((1, 1, d), idx),
                  pl.BlockSpec((1, d, tf), lambda b_, i, f: (layer, 0, f)),
                  pl.BlockSpec((1, d, tf), lambda b_, i, f: (layer, 0, nf + f)),
                  pl.BlockSpec((1, 3, tf), lambda b_, i, f: (layer, 0, f)),
                  pl.BlockSpec((1, 3, tf), lambda b_, i, f: (layer, 0, nf + f)),
                  pl.BlockSpec((1, 1, tf), lambda b_, i, f: (layer, 0, f)),
                  pl.BlockSpec((1, 1, tf), lambda b_, i, f: (layer, 0, nf + f)),
                  pl.BlockSpec((1, tf, d), lambda b_, i, f: (layer, f, 0))],
        out_specs=pl.BlockSpec((1, tm, d), lambda b_, i, f: (b_, i, 0)),
        out_shape=jax.ShapeDtypeStruct((b, L, d), F32),
        scratch_shapes=[pltpu.VMEM((tm + 2 * HALO, d), BF16), pltpu.VMEM((tm, d), F32)],
        compiler_params=_params(("parallel", "parallel", "arbitrary")),
        name="geglu_ffn",
    )(x, x, x, mod, g_pre, g_post, w_up, w_up, dw_w, dw_w, dw_b, dw_b, w_down)


def kernel(x, c, ada_w, ada_b, g_mix_pre, g_mix_post, w_in, short_w, short_b, filt_w1, filt_b1,
           filt_w2, filt_b2, filt_freq, filt_w3, hyena_d, fnet_w, w_out, g_ffn_pre, g_ffn_post,
           w_up, dw_w, dw_b, w_down):
    b, L, d = x.shape
    depth = ada_w.shape[0]
    dh = hyena_d.shape[-1]
    df = w_in.shape[-1] - (HYENA_ORDER + 1) * dh
    cg = df // FNET_GROUPS
    assert b % 2 == 0 and L % (CONV_N2 * SUBLANES) == 0 and L % (FNET_NB * SUBLANES) == 0
    assert dh % LANES == 0 and df % LANES == 0 and LANES % cg == 0
    tm_proj = min(512, L)
    tm_ffn = min(1024, L)
    tf = 256

    row = lambda a: a.reshape(depth, 1, a.shape[-1])
    mod = _modulation(c, ada_w, ada_b).reshape(depth, b, N_MOD, d)

    w1p = jnp.pad(filt_w1, ((0, 0), (0, LANES - filt_w1.shape[1]), (0, 0)))
    kf = _filter_spectra(L, w1p, row(filt_b1), filt_w2, row(filt_b2), row(filt_freq),
                         filt_w3, dh)
    hd = hyena_d.reshape(depth, HYENA_ORDER, 1, dh)

    gpt = LANES // cg
    eye = jnp.eye(gpt, dtype=F32)
    wg = jnp.einsum("ltacd,ae->ltaced", fnet_w.reshape(depth, df // LANES, gpt, cg, cg), eye)
    wg = wg.reshape(depth, df // LANES, LANES, LANES).astype(BF16)

    w_in_b = w_in.astype(BF16)
    w_out_b = w_out.astype(BF16)
    w_up_b = w_up.astype(BF16)
    w_down_b = w_down.astype(BF16)
    g_mix_pre, g_mix_post = row(g_mix_pre), row(g_mix_post)
    g_ffn_pre, g_ffn_post = row(g_ffn_pre), row(g_ffn_post)
    short_b3, dw_b3 = row(short_b), row(dw_b)
    nct = dh // LANES

    for l in range(depth):
        u, pf = _in_projection(x, mod, g_mix_pre, w_in_b, short_w, short_b3, l, tm_proj)
        u4 = u.reshape(2, b // 2, L, u.shape[-1])
        z2 = _long_conv(u4, 0, u4, nct, kf, hd, l, 0)
        yh = _long_conv(z2, 0, u4, 2 * nct, kf, hd, l, 1).reshape(b, L, dh)
        yf = _fnet_mixer(pf, wg, l)
        x = _out_projection(x, yh, yf, mod, g_mix_post, w_out_b, l, tm_proj)
        x = _ffn(x, mod, g_ffn_pre, g_ffn_post, w_up_b, dw_w, dw_b3, w_down_b, l, tm_ffn, tf)
    return x
```

```python
import functools
import math

import numpy as np
import jax
import jax.numpy as jnp
from jax import lax
from jax.experimental import pallas as pl
from jax.experimental.pallas import tpu as pltpu

F32 = jnp.float32
BF16 = jnp.bfloat16

LANES = 128
SUBLANES = 8
HALO = SUBLANES
VMEM_LIMIT = 56 * 1024 * 1024

FNET_GROUPS = 8
HYENA_ORDER = 2
N_DIRS = 2
FILTER_BANDS = 16
FILTER_HIDDEN = 64
DECAY_TARGET = 1e-2
FAST_DECAY_PCT = 0.3
SLOW_DECAY_PCT = 1.5
N_MOD = 6
NORM_EPS = 1e-6
FILTER_EPS = 1e-6

CONV_N2 = 128
FNET_NB = 64
SLAB_PAD = 8
UNROLL = 8


def _params(sem, vmem=VMEM_LIMIT):
    return pltpu.CompilerParams(dimension_semantics=sem, vmem_limit_bytes=vmem)


def _const_spec(shape):
    nd = len(shape)
    return pl.BlockSpec(shape, lambda *_: (0,) * nd, pipeline_mode=pl.Buffered(1))


@functools.lru_cache(maxsize=None)
def _conv_tables(L):
    n2n = CONV_N2
    nh = L // n2n
    n1n = 2 * nh
    N = 2 * L
    k1 = np.arange(n1n)[:, None]
    n1 = np.arange(nh)[None, :]
    tf = np.zeros((n2n, 2 * n1n, 2 * nh), np.float64)
    for n2 in range(n2n):
        th = 2.0 * np.pi * ((k1 * (n2n * n1 + n2)) % N) / N
        c, s = np.cos(th), np.sin(th)
        tf[n2] = np.block([[c, s], [-s, c]])
    ti = np.transpose(tf, (0, 2, 1))
    k2 = np.arange(n2n)[:, None]
    m = np.arange(n2n)[None, :]
    th2 = 2.0 * np.pi * ((k2 * m) % n2n) / n2n
    c2, s2 = np.cos(th2), np.sin(th2)
    m2 = np.block([[c2, s2], [-s2, c2]])
    return (jnp.asarray(tf.reshape(n2n * 2 * n1n, 2 * nh), BF16),
            jnp.asarray(ti.reshape(n2n * 2 * nh, 2 * n1n), BF16),
            jnp.asarray(m2, BF16), jnp.asarray(m2.T, BF16))


@functools.lru_cache(maxsize=None)
def _fnet_tables(L, cg):
    nb = FNET_NB
    na = L // nb
    scale = 1.0 / math.sqrt(L * cg)
    j = np.arange(cg)
    thc = 2.0 * np.pi * ((j[:, None] * j[None, :]) % cg) / cg
    reps = LANES // cg
    cc = np.kron(np.eye(reps), np.cos(thc)) * scale
    sc = np.kron(np.eye(reps), np.sin(thc)) * scale
    cs = np.concatenate([cc, -sc], axis=1)
    k1 = np.arange(na)[:, None]
    n1 = np.arange(na)[None, :]
    tf = np.zeros((nb, 2 * na, 2 * na), np.float64)
    for n2 in range(nb):
        th = 2.0 * np.pi * ((k1 * (nb * n1 + n2)) % L) / L
        c, s = np.cos(th), np.sin(th)
        tf[n2] = np.block([[c, s], [-s, c]])
    k2 = np.arange(nb)[:, None]
    m = np.arange(nb)[None, :]
    th2 = 2.0 * np.pi * ((k2 * m) % nb) / nb
    m4 = np.concatenate([np.cos(th2), np.sin(th2)], axis=1)
    return (jnp.asarray(cs, BF16), jnp.asarray(tf.reshape(nb * 2 * na, 2 * na), BF16),
            jnp.asarray(m4, BF16))


@functools.lru_cache(maxsize=None)
def _filter_features(L):
    pos = np.arange(L, dtype=np.float32)
    bands = np.linspace(1e-4, FILTER_BANDS - 1, FILTER_BANDS, dtype=np.float32)
    t = pos / np.float32(max(L - 1, 1))
    ang = np.float32(2.0 * math.pi / L) * pos[:, None] * bands[None, :]
    z = np.concatenate([t[:, None], np.cos(ang), -np.sin(ang)], axis=-1).astype(np.float32)
    zp = np.zeros((L, LANES), np.float32)
    zp[:, :z.shape[1]] = z
    return jnp.asarray(zp)


def _decay_rates(dh):
    min_decay = math.log(DECAY_TARGET) / SLOW_DECAY_PCT
    max_decay = math.log(DECAY_TARGET) / FAST_DECAY_PCT
    return jnp.abs(jnp.linspace(min_decay, max_decay, dh, dtype=F32)).reshape(1, dh)


def _dot3(a, b):
    a_hi = a.astype(BF16)
    b_hi = b.astype(BF16)
    a_lo = (a - a_hi.astype(F32)).astype(BF16)
    b_lo = (b - b_hi.astype(F32)).astype(BF16)
    dot = functools.partial(jnp.dot, preferred_element_type=F32)
    return dot(a_hi, b_hi) + (dot(a_hi, b_lo) + dot(a_lo, b_hi))


def _modnorm(x, g, scale, shift):
    ms = jnp.mean(x * x, axis=-1, keepdims=True)
    return (x * lax.rsqrt(ms + NORM_EPS) * g) * (1.0 + scale) + shift


def _rmsnorm(y, g):
    ms = jnp.mean(y * y, axis=-1, keepdims=True)
    return y * lax.rsqrt(ms + NORM_EPS) * g


def _halo_rows(xm_ref, xp_ref, xn_ref, g, scale, shift, first, last):
    hm = _modnorm(xm_ref[0], g, scale, shift)
    hp = _modnorm(xp_ref[0], g, scale, shift) * jnp.where(first, 0.0, 1.0)
    hn = _modnorm(xn_ref[0], g, scale, shift) * jnp.where(last, 0.0, 1.0)
    return jnp.concatenate([hp, hm, hn], axis=0).astype(BF16)


def _dwconv3(p, w_ref, b_ref, rows):
    n = p.shape[0]
    prev = pltpu.roll(p, 1, 0)[HALO:HALO + rows]
    nxt = pltpu.roll(p, n - 1, 0)[HALO:HALO + rows]
    cur = p[HALO:HALO + rows]
    return b_ref[...] + prev * w_ref[0:1, :] + cur * w_ref[1:2, :] + nxt * w_ref[2:3, :]


def _mod_kernel(c_ref, w_ref, b_ref, o_ref):
    c = c_ref[...]
    act = c * jax.nn.sigmoid(c)
    o_ref[0] = jnp.dot(act, w_ref[0], preferred_element_type=F32,
                       precision=lax.Precision.HIGHEST) + b_ref[0]


def _modulation(c, ada_w, ada_b):
    depth, d, nm = ada_w.shape
    b = c.shape[0]
    tn = nm // 4
    return pl.pallas_call(
        _mod_kernel,
        grid=(depth, nm // tn),
        in_specs=[pl.BlockSpec((b, d), lambda l, j: (0, 0)),
                  pl.BlockSpec((1, d, tn), lambda l, j: (l, 0, j)),
                  pl.BlockSpec((1, 1, tn), lambda l, j: (l, 0, j))],
        out_specs=pl.BlockSpec((1, b, tn), lambda l, j: (l, 0, j)),
        out_shape=jax.ShapeDtypeStruct((depth, b, nm), F32),
        compiler_params=_params(("parallel", "parallel")),
        name="adaln_modulation",
    )(c, ada_w, ada_b.reshape(depth, 1, nm))


def _stage1(load_x, t_ref, s1_ref, n_iter, rows_out, pitch):
    def body(n2, carry):
        t = t_ref[pl.ds(pl.multiple_of(n2 * rows_out, rows_out), rows_out), :]
        a = jnp.dot(t, load_x(n2), preferred_element_type=F32)
        s1_ref[pl.ds(pl.multiple_of(n2 * pitch, SUBLANES), rows_out), :] = a
        return carry
    lax.fori_loop(0, n_iter, body, 0, unroll=UNROLL)


def _load_slab_column(s1_ref, k1, n1n, n2n, pitch):
    ar = s1_ref[pl.ds(k1, n2n, stride=pitch), :]
    ai = s1_ref[pl.ds(n1n + k1, n2n, stride=pitch), :]
    return jnp.concatenate([ar, ai], axis=0).astype(BF16)


def _filter_kernel(zf_ref, w1_ref, b1_ref, w2_ref, b2_ref, fr_ref, w3f_ref, w3b_ref, dl_ref,
                   tf_ref, m2_ref, kf_ref, h_ref, fwd_ref, bwd_ref, s1_ref, *, L):
    n2n = CONV_N2
    nh = L // n2n
    n1n = 2 * nh
    pitch = 2 * n1n + SLAB_PAD

    @pl.when(pl.program_id(1) == 0)
    def _():
        fr = fr_ref[0]
        h = jnp.sin(fr * (_dot3(zf_ref[...], w1_ref[0]) + b1_ref[0]))
        h_ref[...] = jnp.sin(fr * (_dot3(h, w2_ref[0]) + b2_ref[0]))

    h = h_ref[...]
    hf = _dot3(h, w3f_ref[0])
    hb = _dot3(h, w3b_ref[0])
    pos = lax.broadcasted_iota(jnp.int32, (L, LANES), 0)
    t = pos.astype(F32) / float(max(L - 1, 1))
    decay = jnp.exp(-t * dl_ref[...])
    hf = hf * decay
    hb = jnp.where(pos == 0, 0.0, hb * decay)
    norm = (jnp.sum(jnp.abs(hf), axis=0, keepdims=True)
            + jnp.sum(jnp.abs(hb), axis=0, keepdims=True) + FILTER_EPS)
    inv = 1.0 / (norm * float(2 * L))
    fwd_ref[...] = hf * inv
    bwd_ref[...] = hb * inv

    zeros = jnp.zeros((nh, LANES), F32)
    for src_ref, conj in ((fwd_ref, False), (bwd_ref, True)):
        def load_x(n2, src_ref=src_ref):
            xr = src_ref[pl.ds(n2, nh, stride=n2n), :]
            return jnp.concatenate([xr, zeros], axis=0).astype(BF16)
        _stage1(load_x, tf_ref, s1_ref, n2n, 2 * n1n, pitch)

        def body(k1, carry, conj=conj):
            a = _load_slab_column(s1_ref, k1, n1n, n2n, pitch)
            u = jnp.dot(m2_ref[...], a, preferred_element_type=F32)
            base = pl.multiple_of(k1 * 2 * n2n, 2 * n2n)
            if not conj:
                kf_ref[0, 0, pl.ds(base, 2 * n2n), :] = u
            else:
                kf_ref[0, 0, pl.ds(base, n2n), :] += u[:n2n]
                kf_ref[0, 0, pl.ds(base + n2n, n2n), :] -= u[n2n:]
            return carry
        lax.fori_loop(0, n1n, body, 0, unroll=UNROLL)


def _filter_spectra(L, w1p, b1, w2, b2, freq, w3, dh):
    depth = w1p.shape[0]
    n2n = CONV_N2
    nh = L // n2n
    n1n = 2 * nh
    pitch = 2 * n1n + SLAB_PAD
    tf, _, m2, _ = _conv_tables(L)
    zf = _filter_features(L)
    ct = dh // LANES
    fh = FILTER_HIDDEN
    cols = N_DIRS * ct

    lyr = lambda l, i: (l, 0, 0)
    return pl.pallas_call(
        functools.partial(_filter_kernel, L=L),
        grid=(depth, HYENA_ORDER * ct),
        in_specs=[_const_spec(zf.shape),
                  pl.BlockSpec((1, LANES, fh), lyr), pl.BlockSpec((1, 1, fh), lyr),
                  pl.BlockSpec((1, fh, fh), lyr), pl.BlockSpec((1, 1, fh), lyr),
                  pl.BlockSpec((1, 1, fh), lyr),
                  pl.BlockSpec((1, fh, LANES), lambda l, i: (l, 0, (i // ct) * cols + i % ct)),
                  pl.BlockSpec((1, fh, LANES),
                               lambda l, i: (l, 0, (i // ct) * cols + ct + i % ct)),
                  pl.BlockSpec((1, LANES), lambda l, i: (0, i % ct)),
                  _const_spec(tf.shape), _const_spec(m2.shape)],
        out_specs=pl.BlockSpec((1, 1, n1n * 2 * n2n, LANES),
                               lambda l, i: (l, i // ct, 0, i % ct)),
        out_shape=jax.ShapeDtypeStruct((depth, HYENA_ORDER, n1n * 2 * n2n, dh), F32),
        scratch_shapes=[pltpu.VMEM((L, fh), F32),
                        pltpu.VMEM((L, LANES), F32), pltpu.VMEM((L, LANES), F32),
                        pltpu.VMEM((n2n * pitch, LANES), F32)],
        compiler_params=_params(("parallel", "arbitrary")),
        name="hyena_filter_spectrum",
    )(zf, w1p, b1, w2, b2, freq, w3, w3, _decay_rates(dh), tf, m2)


def _conv_kernel(z_ref, g_ref, kf_ref, d_ref, tf_ref, ti_ref, m2_ref, m2i_ref, o_ref, s1_ref,
                 *, L):
    n2n = CONV_N2
    nh = L // n2n
    n1n = 2 * nh
    pitch = 2 * n1n + SLAB_PAD

    def load_x(n2):
        zr = z_ref[0, 0, pl.ds(n2, nh, stride=n2n), :]
        zi = z_ref[1, 0, pl.ds(n2, nh, stride=n2n), :]
        return jnp.concatenate([zr, zi], axis=0).astype(BF16)
    _stage1(load_x, tf_ref, s1_ref, n2n, 2 * n1n, pitch)

    def mid(j, carry):
        cols = [j * UNROLL + i for i in range(UNROLL)]
        a = [_load_slab_column(s1_ref, k1, n1n, n2n, pitch) for k1 in cols]
        w = []
        for k1, ak in zip(cols, a):
            u = jnp.dot(m2_ref[...], ak, preferred_element_type=F32)
            ur, ui = u[:n2n], u[n2n:]
            base = pl.multiple_of(k1 * 2 * n2n, 2 * n2n)
            kr = kf_ref[0, 0, pl.ds(base, n2n), :]
            ki = kf_ref[0, 0, pl.ds(base + n2n, n2n), :]
            v = jnp.concatenate([ur * kr - ui * ki, ur * ki + ui * kr], axis=0).astype(BF16)
            w.append(jnp.dot(m2i_ref[...], v, preferred_element_type=F32))
        for k1, wk in zip(cols, w):
            s1_ref[pl.ds(k1, n2n, stride=pitch), :] = wk[:n2n]
            s1_ref[pl.ds(n1n + k1, n2n, stride=pitch), :] = wk[n2n:]
        return carry
    lax.fori_loop(0, n1n // UNROLL, mid, 0)

    d = d_ref[0, 0]

    def last(n2, carry):
        w = s1_ref[pl.ds(pl.multiple_of(n2 * pitch, SUBLANES), 2 * n1n), :].astype(BF16)
        t = ti_ref[pl.ds(pl.multiple_of(n2 * 2 * nh, 2 * nh), 2 * nh), :]
        y = jnp.dot(t, w, preferred_element_type=F32)
        for half in range(2):
            zz = z_ref[half, 0, pl.ds(n2, nh, stride=n2n), :]
            gg = g_ref[half, 0, pl.ds(n2, nh, stride=n2n), :]
            o_ref[half, 0, pl.ds(n2, nh, stride=n2n), :] = gg * (y[half * nh:(half + 1) * nh]
                                                                + d * zz)
        return carry
    lax.fori_loop(0, n2n, last, 0, unroll=UNROLL)


def _long_conv(z, z_col0, gate, gate_col0, kf, d, layer, order):
    _, bh, L, _ = z.shape
    dh = kf.shape[-1]
    ct = dh // LANES
    n2n = CONV_N2
    nh = L // n2n
    n1n = 2 * nh
    pitch = 2 * n1n + SLAB_PAD
    tf, ti, m2, m2i = _conv_tables(L)
    blk = (2, 1, L, LANES)
    return pl.pallas_call(
        functools.partial(_conv_kernel, L=L),
        grid=(ct, bh),
        in_specs=[pl.BlockSpec(blk, lambda c, p: (0, p, 0, z_col0 + c)),
                  pl.BlockSpec(blk, lambda c, p: (0, p, 0, gate_col0 + c)),
                  pl.BlockSpec((1, 1, n1n * 2 * n2n, LANES), lambda c, p: (layer, order, 0, c),
                               pipeline_mode=pl.Buffered(1)),
                  pl.BlockSpec((1, 1, 1, LANES), lambda c, p: (layer, order, 0, c)),
                  _const_spec(tf.shape), _const_spec(ti.shape),
                  _const_spec(m2.shape), _const_spec(m2i.shape)],
        out_specs=pl.BlockSpec(blk, lambda c, p: (0, p, 0, c)),
        out_shape=jax.ShapeDtypeStruct((2, bh, L, dh), F32),
        scratch_shapes=[pltpu.VMEM((n2n * pitch, LANES), F32)],
        compiler_params=_params(("parallel", "parallel")),
        name="hyena_long_conv",
    )(z, gate, kf, d, tf, ti, m2, m2i)


def _fnet_kernel(p_ref, cs_ref, tf_ref, m4_ref, wg_ref, o_ref, a_ref, b_ref, s1_ref, *, L):
    nb = FNET_NB
    na = L // nb
    pitch = 2 * na + SLAB_PAD
    ab = jnp.dot(p_ref[0].astype(BF16), cs_ref[...], preferred_element_type=F32)
    a_ref[...] = ab[:, :LANES]
    b_ref[...] = ab[:, LANES:]

    def load_x(n2):
        zr = a_ref[pl.ds(n2, na, stride=nb), :]
        zi = b_ref[pl.ds(n2, na, stride=nb), :]
        return jnp.concatenate([zr, zi], axis=0).astype(BF16)
    _stage1(load_x, tf_ref, s1_ref, nb, 2 * na, pitch)

    def body(k1, carry):
        a = _load_slab_column(s1_ref, k1, na, nb, pitch)
        r = jnp.dot(m4_ref[...], a, preferred_element_type=F32)
        y = jnp.dot(r.astype(BF16), wg_ref[0, 0], preferred_element_type=F32)
        o_ref[0, pl.ds(k1, nb, stride=na), :] = y
        return carry
    lax.fori_loop(0, na, body, 0, unroll=UNROLL)


def _fnet_mixer(pf, wg, layer):
    b, L, df = pf.shape
    cg = df // FNET_GROUPS
    cs, tf, m4 = _fnet_tables(L, cg)
    nb = FNET_NB
    na = L // nb
    pitch = 2 * na + SLAB_PAD
    ct = df // LANES
    return pl.pallas_call(
        functools.partial(_fnet_kernel, L=L),
        grid=(b, ct),
        in_specs=[pl.BlockSpec((1, L, LANES), lambda i, c: (i, 0, c)),
                  _const_spec(cs.shape), _const_spec(tf.shape), _const_spec(m4.shape),
                  pl.BlockSpec((1, 1, LANES, LANES), lambda i, c: (layer, c, 0, 0))],
        out_specs=pl.BlockSpec((1, L, LANES), lambda i, c: (i, 0, c)),
        out_shape=jax.ShapeDtypeStruct((b, L, df), F32),
        scratch_shapes=[pltpu.VMEM((L, LANES), F32), pltpu.VMEM((L, LANES), F32),
                        pltpu.VMEM((nb * pitch, LANES), F32)],
        compiler_params=_params(("parallel", "parallel")),
        name="fnet_mixer",
    )(pf, cs, tf, m4, wg)


def _inproj_kernel(xm_ref, xp_ref, xn_ref, mod_ref, g_ref, w_ref, sw_ref, sb_ref,
                   u_ref, pf_ref, *, tm, dconv):
    i = pl.program_id(1)
    shift, scale = mod_ref[0, 0, 0:1, :], mod_ref[0, 0, 1:2, :]
    h = _halo_rows(xm_ref, xp_ref, xn_ref, g_ref[0], scale, shift,
                   i == 0, i == pl.num_programs(1) - 1)
    p = jnp.dot(h, w_ref[0], preferred_element_type=F32)
    u_ref[0] = _dwconv3(p[:, :dconv], sw_ref.at[0], sb_ref.at[0], tm)
    pf_ref[0] = p[HALO:HALO + tm, dconv:]


def _row_halo_specs(tm, L, d):
    nblk = L // HALO
    step = tm // HALO
    return [pl.BlockSpec((1, tm, d), lambda b, i: (b, i, 0)),
            pl.BlockSpec((1, HALO, d), lambda b, i: (b, jnp.maximum(i * step - 1, 0), 0)),
            pl.BlockSpec((1, HALO, d), lambda b, i: (b, jnp.minimum((i + 1) * step, nblk - 1), 0))]


def _in_projection(x, mod, g_pre, w_in, short_w, short_b, layer, tm):
    b, L, d = x.shape
    e = w_in.shape[-1]
    dconv = short_w.shape[-1]
    return pl.pallas_call(
        functools.partial(_inproj_kernel, tm=tm, dconv=dconv),
        grid=(b, L // tm),
        in_specs=_row_halo_specs(tm, L, d) + [
            pl.BlockSpec((1, 1, N_MOD, d), lambda b_, i: (layer, b_, 0, 0)),
            pl.BlockSpec((1, 1, d), lambda b_, i: (layer, 0, 0)),
            pl.BlockSpec((1, d, e), lambda b_, i: (layer, 0, 0)),
            pl.BlockSpec((1, 3, dconv), lambda b_, i: (layer, 0, 0)),
            pl.BlockSpec((1, 1, dconv), lambda b_, i: (layer, 0, 0))],
        out_specs=[pl.BlockSpec((1, tm, dconv), lambda b_, i: (b_, i, 0)),
                   pl.BlockSpec((1, tm, e - dconv), lambda b_, i: (b_, i, 0))],
        out_shape=[jax.ShapeDtypeStruct((b, L, dconv), F32),
                   jax.ShapeDtypeStruct((b, L, e - dconv), F32)],
        compiler_params=_params(("parallel", "parallel")),
        name="in_projection",
    )(x, x, x, mod, g_pre, w_in, short_w, short_b)


def _outproj_kernel(x_ref, yh_ref, yf_ref, mod_ref, g_ref, wh_ref, wf_ref, o_ref):
    y = (jnp.dot(yh_ref[0].astype(BF16), wh_ref[0], preferred_element_type=F32)
         + jnp.dot(yf_ref[0].astype(BF16), wf_ref[0], preferred_element_type=F32))
    gate = mod_ref[0, 0, 2:3, :]
    o_ref[0] = x_ref[0] + gate * _rmsnorm(y, g_ref[0])


def _out_projection(x, yh, yf, mod, g_post, w_out, layer, tm):
    b, L, d = x.shape
    dh = yh.shape[-1]
    df = yf.shape[-1]
    return pl.pallas_call(
        _outproj_kernel,
        grid=(b, L // tm),
        in_specs=[pl.BlockSpec((1, tm, d), lambda b_, i: (b_, i, 0)),
                  pl.BlockSpec((1, tm, dh), lambda b_, i: (b_, i, 0)),
                  pl.BlockSpec((1, tm, df), lambda b_, i: (b_, i, 0)),
                  pl.BlockSpec((1, 1, N_MOD, d), lambda b_, i: (layer, b_, 0, 0)),
                  pl.BlockSpec((1, 1, d), lambda b_, i: (layer, 0, 0)),
                  pl.BlockSpec((1, dh, d), lambda b_, i: (layer, 0, 0)),
                  pl.BlockSpec((1, df, d), lambda b_, i: (layer, dh // df, 0))],
        out_specs=pl.BlockSpec((1, tm, d), lambda b_, i: (b_, i, 0)),
        out_shape=jax.ShapeDtypeStruct((b, L, d), F32),
        compiler_params=_params(("parallel", "parallel")),
        name="out_projection",
    )(x, yh, yf, mod, g_post, w_out, w_out)


def _ffn_kernel(xm_ref, xp_ref, xn_ref, mod_ref, gpre_ref, gpost_ref, wu_ref, cw_ref, cb_ref,
                wd_ref, o_ref, h_ref, *, tm, tf, nf):
    i = pl.program_id(1)
    shift, scale = mod_ref[0, 0, 3:4, :], mod_ref[0, 0, 4:5, :]
    h_ref[...] = _halo_rows(xm_ref, xp_ref, xn_ref, gpre_ref[0], scale, shift,
                            i == 0, i == pl.num_programs(1) - 1)
    dff = nf * tf

    def up(f):
        h = h_ref[...]
        return [jnp.dot(h, wu_ref[0, :, pl.ds(c0, tf)], preferred_element_type=F32)
                for c0 in (f * tf, dff + f * tf)]

    def gate_act(f, pa, pb):
        a = _dwconv3(pa, cw_ref.at[0, :, pl.ds(f * tf, tf)], cb_ref.at[0, :, pl.ds(f * tf, tf)], tm)
        b = _dwconv3(pb, cw_ref.at[0, :, pl.ds(dff + f * tf, tf)],
                     cb_ref.at[0, :, pl.ds(dff + f * tf, tf)], tm)
        return (0.5 * a * (1.0 + lax.erf(a * (1.0 / math.sqrt(2.0)))) * b).astype(BF16)

    ups, acts, acc = {}, {}, None
    for s in range(nf + 2):
        if s < nf:
            ups[s] = up(s)
        if 0 <= s - 1 < nf:
            acts[s - 1] = gate_act(s - 1, *ups.pop(s - 1))
        if 0 <= s - 2 < nf:
            y = jnp.dot(acts.pop(s - 2), wd_ref[0, pl.ds((s - 2) * tf, tf), :],
                        preferred_element_type=F32)
            acc = y if acc is None else acc + y
    gate = mod_ref[0, 0, 5:6, :]
    o_ref[0] = xm_ref[0] + gate * _rmsnorm(acc, gpost_ref[0])


def _ffn(x, mod, g_pre, g_post, w_up, dw_w, dw_b, w_down, layer, tm, tf):
    b, L, d = x.shape
    dff = w_down.shape[1]
    idx = lambda b_, i: (layer, 0, 0)
    once = dict(pipeline_mode=pl.Buffered(1))
    return pl.pallas_call(
        functools.partial(_ffn_kernel, tm=tm, tf=tf, nf=dff // tf),
        grid=(b, L // tm),
        in_specs=_row_halo_specs(tm, L, d) + [
            pl.BlockSpec((1, 1, N_MOD, d), lambda b_, i: (layer, b_, 0, 0)),
            pl.BlockSpec((1, 1, d), idx), pl.BlockSpec((1, 1, d), idx),
            pl.BlockSpec((1, d, 2 * dff), idx, **once),
            pl.BlockSpec((1, 3, 2 * dff), idx), pl.BlockSpec((1, 1, 2 * dff), idx),
            pl.BlockSpec((1, dff, d), idx, **once)],
        out_specs=pl.BlockSpec((1, tm, d), lambda b_, i: (b_, i, 0)),
        out_shape=jax.ShapeDtypeStruct((b, L, d), F32),
        scratch_shapes=[pltpu.VMEM((tm + 2 * HALO, d), BF16)],
        compiler_params=_params(("parallel", "parallel")),
        name="geglu_ffn",
    )(x, x, x, mod, g_pre, g_post, w_up, dw_w, dw_b, w_down)


def kernel(x, c, ada_w, ada_b, g_mix_pre, g_mix_post, w_in, short_w, short_b, filt_w1, filt_b1,
           filt_w2, filt_b2, filt_freq, filt_w3, hyena_d, fnet_w, w_out, g_ffn_pre, g_ffn_post,
           w_up, dw_w, dw_b, w_down):
    b, L, d = x.shape
    depth = ada_w.shape[0]
    dh = hyena_d.shape[-1]
    df = w_in.shape[-1] - (HYENA_ORDER + 1) * dh
    cg = df // FNET_GROUPS
    assert b % 2 == 0 and L % (CONV_N2 * SUBLANES) == 0 and L % (FNET_NB * SUBLANES) == 0
    assert dh % LANES == 0 and df % LANES == 0 and LANES % cg == 0
    tm_proj = min(512, L)
    tm_ffn = min(512, L)
    tf = 256

    row = lambda a: a.reshape(depth, 1, a.shape[-1])
    mod = _modulation(c, ada_w, ada_b).reshape(depth, b, N_MOD, d)

    w1p = jnp.pad(filt_w1, ((0, 0), (0, LANES - filt_w1.shape[1]), (0, 0)))
    kf = _filter_spectra(L, w1p, row(filt_b1), filt_w2, row(filt_b2), row(filt_freq),
                         filt_w3, dh)
    hd = hyena_d.reshape(depth, HYENA_ORDER, 1, dh)

    gpt = LANES // cg
    eye = jnp.eye(gpt, dtype=F32)
    wg = jnp.einsum("ltacd,ae->ltaced", fnet_w.reshape(depth, df // LANES, gpt, cg, cg), eye)
    wg = wg.reshape(depth, df // LANES, LANES, LANES).astype(BF16)

    w_in_b = w_in.astype(BF16)
    w_out_b = w_out.astype(BF16)
    w_up_b = w_up.astype(BF16)
    w_down_b = w_down.astype(BF16)
    g_mix_pre, g_mix_post = row(g_mix_pre), row(g_mix_post)
    g_ffn_pre, g_ffn_post = row(g_ffn_pre), row(g_ffn_post)
    short_b3, dw_b3 = row(short_b), row(dw_b)
    nct = dh // LANES

    for l in range(depth):
        u, pf = _in_projection(x, mod, g_mix_pre, w_in_b, short_w, short_b3, l, tm_proj)
        u4 = u.reshape(2, b // 2, L, u.shape[-1])
        z2 = _long_conv(u4, 0, u4, nct, kf, hd, l, 0)
        yh = _long_conv(z2, 0, u4, 2 * nct, kf, hd, l, 1).reshape(b, L, dh)
        yf = _fnet_mixer(pf, wg, l)
        x = _out_projection(x, yh, yf, mod, g_mix_post, w_out_b, l, tm_proj)
        x = _ffn(x, mod, g_ffn_pre, g_ffn_post, w_up_b, dw_w, dw_b3, w_down_b, l, tm_ffn, tf)
    return x
```

```python
import functools
import math

import numpy as np
import jax
import jax.numpy as jnp
from jax import lax
from jax.experimental import pallas as pl
from jax.experimental.pallas import tpu as pltpu

F32 = jnp.float32
BF16 = jnp.bfloat16

LANES = 128
SUBLANES = 8
HALO = SUBLANES
VMEM_LIMIT = 56 * 1024 * 1024

FNET_GROUPS = 8
HYENA_ORDER = 2
N_DIRS = 2
FILTER_BANDS = 16
FILTER_HIDDEN = 64
DECAY_TARGET = 1e-2
FAST_DECAY_PCT = 0.3
SLOW_DECAY_PCT = 1.5
N_MOD = 6
NORM_EPS = 1e-6
FILTER_EPS = 1e-6

CONV_N2 = 128
FNET_NB = 64
SLAB_PAD = 8
UNROLL = 8


def _params(sem, vmem=VMEM_LIMIT):
    return pltpu.CompilerParams(dimension_semantics=sem, vmem_limit_bytes=vmem)


def _const_spec(shape):
    nd = len(shape)
    return pl.BlockSpec(shape, lambda *_: (0,) * nd, pipeline_mode=pl.Buffered(1))


@functools.lru_cache(maxsize=None)
def _conv_tables(L):
    n2n = CONV_N2
    nh = L // n2n
    n1n = 2 * nh
    N = 2 * L
    k1 = np.arange(n1n)[:, None]
    n1 = np.arange(nh)[None, :]
    tf = np.zeros((n2n, 2 * n1n, 2 * nh), np.float64)
    for n2 in range(n2n):
        th = 2.0 * np.pi * ((k1 * (n2n * n1 + n2)) % N) / N
        c, s = np.cos(th), np.sin(th)
        tf[n2] = np.block([[c, s], [-s, c]])
    ti = np.transpose(tf, (0, 2, 1))
    k2 = np.arange(n2n)[:, None]
    m = np.arange(n2n)[None, :]
    th2 = 2.0 * np.pi * ((k2 * m) % n2n) / n2n
    c2, s2 = np.cos(th2), np.sin(th2)
    m2 = np.block([[c2, s2], [-s2, c2]])
    return (jnp.asarray(tf.reshape(n2n * 2 * n1n, 2 * nh), BF16),
            jnp.asarray(ti.reshape(n2n * 2 * nh, 2 * n1n), BF16),
            jnp.asarray(m2, BF16), jnp.asarray(m2.T, BF16))


@functools.lru_cache(maxsize=None)
def _fnet_tables(L, cg):
    nb = FNET_NB
    na = L // nb
    scale = 1.0 / math.sqrt(L * cg)
    j = np.arange(cg)
    thc = 2.0 * np.pi * ((j[:, None] * j[None, :]) % cg) / cg
    reps = LANES // cg
    cc = np.kron(np.eye(reps), np.cos(thc)) * scale
    sc = np.kron(np.eye(reps), np.sin(thc)) * scale
    cs = np.concatenate([cc, -sc], axis=1)
    k1 = np.arange(na)[:, None]
    n1 = np.arange(na)[None, :]
    tf = np.zeros((nb, 2 * na, 2 * na), np.float64)
    for n2 in range(nb):
        th = 2.0 * np.pi * ((k1 * (nb * n1 + n2)) % L) / L
        c, s = np.cos(th), np.sin(th)
        tf[n2] = np.block([[c, s], [-s, c]])
    k2 = np.arange(nb)[:, None]
    m = np.arange(nb)[None, :]
    th2 = 2.0 * np.pi * ((k2 * m) % nb) / nb
    m4 = np.concatenate([np.cos(th2), np.sin(th2)], axis=1)
    return (jnp.asarray(cs, BF16), jnp.asarray(tf.reshape(nb * 2 * na, 2 * na), BF16),
            jnp.asarray(m4, BF16))


@functools.lru_cache(maxsize=None)
def _filter_features(L):
    pos = np.arange(L, dtype=np.float32)
    bands = np.linspace(1e-4, FILTER_BANDS - 1, FILTER_BANDS, dtype=np.float32)
    t = pos / np.float32(max(L - 1, 1))
    ang = np.float32(2.0 * math.pi / L) * pos[:, None] * bands[None, :]
    z = np.concatenate([t[:, None], np.cos(ang), -np.sin(ang)], axis=-1).astype(np.float32)
    zp = np.zeros((L, LANES), np.float32)
    zp[:, :z.shape[1]] = z
    return jnp.asarray(zp)


def _decay_rates(dh):
    min_decay = math.log(DECAY_TARGET) / SLOW_DECAY_PCT
    max_decay = math.log(DECAY_TARGET) / FAST_DECAY_PCT
    return jnp.abs(jnp.linspace(min_decay, max_decay, dh, dtype=F32)).reshape(1, dh)


def _loop(n, body):
    for i in range(n):
        body(i, 0)


def _aligned(i, m):
    return i if isinstance(i, int) else pl.multiple_of(i, m)


def _dot3(a, b):
    a_hi = a.astype(BF16)
    b_hi = b.astype(BF16)
    a_lo = (a - a_hi.astype(F32)).astype(BF16)
    b_lo = (b - b_hi.astype(F32)).astype(BF16)
    dot = functools.partial(jnp.dot, preferred_element_type=F32)
    return dot(a_hi, b_hi) + (dot(a_hi, b_lo) + dot(a_lo, b_hi))


def _modnorm(x, g, scale, shift):
    ms = jnp.mean(x * x, axis=-1, keepdims=True)
    return (x * lax.rsqrt(ms + NORM_EPS) * g) * (1.0 + scale) + shift


def _rmsnorm(y, g):
    ms = jnp.mean(y * y, axis=-1, keepdims=True)
    return y * lax.rsqrt(ms + NORM_EPS) * g


def _halo_rows(xm_ref, xp_ref, xn_ref, g, scale, shift, first, last):
    hm = _modnorm(xm_ref[0], g, scale, shift)
    hp = _modnorm(xp_ref[0], g, scale, shift) * jnp.where(first, 0.0, 1.0)
    hn = _modnorm(xn_ref[0], g, scale, shift) * jnp.where(last, 0.0, 1.0)
    return jnp.concatenate([hp, hm, hn], axis=0).astype(BF16)


def _dwconv3(p, w_ref, b_ref, rows):
    n = p.shape[0]
    prev = pltpu.roll(p, 1, 0)[HALO:HALO + rows]
    nxt = pltpu.roll(p, n - 1, 0)[HALO:HALO + rows]
    cur = p[HALO:HALO + rows]
    return b_ref[...] + prev * w_ref[0:1, :] + cur * w_ref[1:2, :] + nxt * w_ref[2:3, :]


def _mod_kernel(c_ref, w_ref, b_ref, o_ref):
    c = c_ref[...]
    act = c * jax.nn.sigmoid(c)
    o_ref[0] = jnp.dot(act, w_ref[0], preferred_element_type=F32,
                       precision=lax.Precision.HIGHEST) + b_ref[0]


def _modulation(c, ada_w, ada_b):
    depth, d, nm = ada_w.shape
    b = c.shape[0]
    tn = nm // 4
    return pl.pallas_call(
        _mod_kernel,
        grid=(depth, nm // tn),
        in_specs=[pl.BlockSpec((b, d), lambda l, j: (0, 0)),
                  pl.BlockSpec((1, d, tn), lambda l, j: (l, 0, j)),
                  pl.BlockSpec((1, 1, tn), lambda l, j: (l, 0, j))],
        out_specs=pl.BlockSpec((1, b, tn), lambda l, j: (l, 0, j)),
        out_shape=jax.ShapeDtypeStruct((depth, b, nm), F32),
        compiler_params=_params(("parallel", "parallel")),
        name="adaln_modulation",
    )(c, ada_w, ada_b.reshape(depth, 1, nm))


def _stage1(load_x, t_ref, s1_ref, n_iter, rows_out, pitch):
    def body(n2, carry):
        t = t_ref[pl.ds(_aligned(n2 * rows_out, rows_out), rows_out), :]
        a = jnp.dot(t, load_x(n2), preferred_element_type=F32)
        s1_ref[pl.ds(_aligned(n2 * pitch, SUBLANES), rows_out), :] = a
        return carry
    _loop(n_iter, body)


def _load_slab_column(s1_ref, k1, n1n, n2n, pitch):
    ar = s1_ref[pl.ds(k1, n2n, stride=pitch), :]
    ai = s1_ref[pl.ds(n1n + k1, n2n, stride=pitch), :]
    return jnp.concatenate([ar, ai], axis=0).astype(BF16)


def _filter_kernel(zf_ref, w1_ref, b1_ref, w2_ref, b2_ref, fr_ref, w3f_ref, w3b_ref, dl_ref,
                   tf_ref, m2_ref, kf_ref, h_ref, fwd_ref, bwd_ref, s1_ref, *, L):
    n2n = CONV_N2
    nh = L // n2n
    n1n = 2 * nh
    pitch = 2 * n1n + SLAB_PAD

    @pl.when(pl.program_id(1) == 0)
    def _():
        fr = fr_ref[0]
        h = jnp.sin(fr * (_dot3(zf_ref[...], w1_ref[0]) + b1_ref[0]))
        h_ref[...] = jnp.sin(fr * (_dot3(h, w2_ref[0]) + b2_ref[0]))

    h = h_ref[...]
    hf = _dot3(h, w3f_ref[0])
    hb = _dot3(h, w3b_ref[0])
    pos = lax.broadcasted_iota(jnp.int32, (L, LANES), 0)
    t = pos.astype(F32) / float(max(L - 1, 1))
    decay = jnp.exp(-t * dl_ref[...])
    hf = hf * decay
    hb = jnp.where(pos == 0, 0.0, hb * decay)
    norm = (jnp.sum(jnp.abs(hf), axis=0, keepdims=True)
            + jnp.sum(jnp.abs(hb), axis=0, keepdims=True) + FILTER_EPS)
    inv = 1.0 / (norm * float(2 * L))
    fwd_ref[...] = hf * inv
    bwd_ref[...] = hb * inv

    zeros = jnp.zeros((nh, LANES), F32)
    for src_ref, conj in ((fwd_ref, False), (bwd_ref, True)):
        def load_x(n2, src_ref=src_ref):
            xr = src_ref[pl.ds(n2, nh, stride=n2n), :]
            return jnp.concatenate([xr, zeros], axis=0).astype(BF16)
        _stage1(load_x, tf_ref, s1_ref, n2n, 2 * n1n, pitch)

        def body(k1, carry, conj=conj):
            a = _load_slab_column(s1_ref, k1, n1n, n2n, pitch)
            u = jnp.dot(m2_ref[...], a, preferred_element_type=F32)
            base = _aligned(k1 * 2 * n2n, 2 * n2n)
            if not conj:
                kf_ref[0, 0, pl.ds(base, 2 * n2n), :] = u
            else:
                kf_ref[0, 0, pl.ds(base, n2n), :] += u[:n2n]
                kf_ref[0, 0, pl.ds(base + n2n, n2n), :] -= u[n2n:]
            return carry
        _loop(n1n, body)


def _filter_spectra(L, w1p, b1, w2, b2, freq, w3, dh):
    depth = w1p.shape[0]
    n2n = CONV_N2
    nh = L // n2n
    n1n = 2 * nh
    pitch = 2 * n1n + SLAB_PAD
    tf, _, m2, _ = _conv_tables(L)
    zf = _filter_features(L)
    ct = dh // LANES
    fh = FILTER_HIDDEN
    cols = N_DIRS * ct

    lyr = lambda l, i: (l, 0, 0)
    return pl.pallas_call(
        functools.partial(_filter_kernel, L=L),
        grid=(depth, HYENA_ORDER * ct),
        in_specs=[_const_spec(zf.shape),
                  pl.BlockSpec((1, LANES, fh), lyr), pl.BlockSpec((1, 1, fh), lyr),
                  pl.BlockSpec((1, fh, fh), lyr), pl.BlockSpec((1, 1, fh), lyr),
                  pl.BlockSpec((1, 1, fh), lyr),
                  pl.BlockSpec((1, fh, LANES), lambda l, i: (l, 0, (i // ct) * cols + i % ct)),
                  pl.BlockSpec((1, fh, LANES),
                               lambda l, i: (l, 0, (i // ct) * cols + ct + i % ct)),
                  pl.BlockSpec((1, LANES), lambda l, i: (0, i % ct)),
                  _const_spec(tf.shape), _const_spec(m2.shape)],
        out_specs=pl.BlockSpec((1, 1, n1n * 2 * n2n, LANES),
                               lambda l, i: (l, i // ct, 0, i % ct)),
        out_shape=jax.ShapeDtypeStruct((depth, HYENA_ORDER, n1n * 2 * n2n, dh), F32),
        scratch_shapes=[pltpu.VMEM((L, fh), F32),
                        pltpu.VMEM((L, LANES), F32), pltpu.VMEM((L, LANES), F32),
                        pltpu.VMEM((n2n * pitch, LANES), F32)],
        compiler_params=_params(("parallel", "arbitrary")),
        name="hyena_filter_spectrum",
    )(zf, w1p, b1, w2, b2, freq, w3, w3, _decay_rates(dh), tf, m2)


def _conv_kernel(z_ref, g_ref, kf_ref, d_ref, tf_ref, ti_ref, m2_ref, m2i_ref, o_ref, s1_ref,
                 *, L):
    n2n = CONV_N2
    nh = L // n2n
    n1n = 2 * nh
    pitch = 2 * n1n + SLAB_PAD

    def rows(n2):
        return pl.ds(n2 * nh, nh)

    def load_x(n2):
        return jnp.concatenate([z_ref[0, 0, 0, rows(n2), :], z_ref[0, 1, 0, rows(n2), :]],
                               axis=0).astype(BF16)
    _stage1(load_x, tf_ref, s1_ref, n2n, 2 * n1n, pitch)

    def mid(j, carry):
        cols = [j * UNROLL + i for i in range(UNROLL)]
        a = [_load_slab_column(s1_ref, k1, n1n, n2n, pitch) for k1 in cols]
        w = []
        for k1, ak in zip(cols, a):
            u = jnp.dot(m2_ref[...], ak, preferred_element_type=F32)
            ur, ui = u[:n2n], u[n2n:]
            base = _aligned(k1 * 2 * n2n, 2 * n2n)
            kr = kf_ref[0, 0, pl.ds(base, n2n), :]
            ki = kf_ref[0, 0, pl.ds(base + n2n, n2n), :]
            v = jnp.concatenate([ur * kr - ui * ki, ur * ki + ui * kr], axis=0).astype(BF16)
            w.append(jnp.dot(m2i_ref[...], v, preferred_element_type=F32))
        for k1, wk in zip(cols, w):
            s1_ref[pl.ds(k1, n2n, stride=pitch), :] = wk[:n2n]
            s1_ref[pl.ds(n1n + k1, n2n, stride=pitch), :] = wk[n2n:]
        return carry
    _loop(n1n // UNROLL, mid)

    d = d_ref[0, 0]

    def last(n2, carry):
        w = s1_ref[pl.ds(_aligned(n2 * pitch, SUBLANES), 2 * n1n), :].astype(BF16)
        t = ti_ref[pl.ds(_aligned(n2 * 2 * nh, 2 * nh), 2 * nh), :]
        y = jnp.dot(t, w, preferred_element_type=F32)
        for half in range(2):
            zz = z_ref[0, half, 0, rows(n2), :]
            gg = g_ref[0, half, 0, rows(n2), :]
            o_ref[0, half, 0, rows(n2), :] = gg * (y[half * nh:(half + 1) * nh] + d * zz)
        return carry
    _loop(n2n, last)


def _long_conv(z, z_col0, gate, gate_col0, kf, d, layer, order):
    _, _, bh, L, _ = z.shape
    dh = kf.shape[-1]
    ct = dh // LANES
    n2n = CONV_N2
    nh = L // n2n
    n1n = 2 * nh
    pitch = 2 * n1n + SLAB_PAD
    tf, ti, m2, m2i = _conv_tables(L)
    blk = (1, 2, 1, L, LANES)
    return pl.pallas_call(
        functools.partial(_conv_kernel, L=L),
        grid=(ct, bh),
        in_specs=[pl.BlockSpec(blk, lambda c, p: (z_col0 + c, 0, p, 0, 0)),
                  pl.BlockSpec(blk, lambda c, p: (gate_col0 + c, 0, p, 0, 0)),
                  pl.BlockSpec((1, 1, n1n * 2 * n2n, LANES), lambda c, p: (layer, order, 0, c),
                               pipeline_mode=pl.Buffered(1)),
                  pl.BlockSpec((1, 1, 1, LANES), lambda c, p: (layer, order, 0, c)),
                  _const_spec(tf.shape), _const_spec(ti.shape),
                  _const_spec(m2.shape), _const_spec(m2i.shape)],
        out_specs=pl.BlockSpec(blk, lambda c, p: (c, 0, p, 0, 0)),
        out_shape=jax.ShapeDtypeStruct((ct, 2, bh, L, LANES), F32),
        scratch_shapes=[pltpu.VMEM((n2n * pitch, LANES), F32)],
        compiler_params=_params(("parallel", "parallel")),
        name="hyena_long_conv",
    )(z, gate, kf, d, tf, ti, m2, m2i)


def _fnet_kernel(p_ref, cs_ref, tf_ref, m4_ref, wg_ref, o_ref, a_ref, b_ref, s1_ref, *, L):
    nb = FNET_NB
    na = L // nb
    pitch = 2 * na + SLAB_PAD
    ab = jnp.dot(p_ref[0, 0].astype(BF16), cs_ref[...], preferred_element_type=F32)
    a_ref[...] = ab[:, :LANES]
    b_ref[...] = ab[:, LANES:]

    def load_x(n2):
        return jnp.concatenate([a_ref[pl.ds(n2 * na, na), :], b_ref[pl.ds(n2 * na, na), :]],
                               axis=0).astype(BF16)
    _stage1(load_x, tf_ref, s1_ref, nb, 2 * na, pitch)

    def body(k1, carry):
        a = _load_slab_column(s1_ref, k1, na, nb, pitch)
        r = jnp.dot(m4_ref[...], a, preferred_element_type=F32)
        y = jnp.dot(r.astype(BF16), wg_ref[0, 0], preferred_element_type=F32)
        o_ref[0, 0, pl.ds(k1 * nb, nb), :] = y
        return carry
    _loop(na, body)


def _fnet_mixer(pf, wg, layer):
    ct, b, L, _ = pf.shape
    cg = ct * LANES // FNET_GROUPS
    cs, tf, m4 = _fnet_tables(L, cg)
    nb = FNET_NB
    na = L // nb
    pitch = 2 * na + SLAB_PAD
    return pl.pallas_call(
        functools.partial(_fnet_kernel, L=L),
        grid=(b, ct),
        in_specs=[pl.BlockSpec((1, 1, L, LANES), lambda i, c: (c, i, 0, 0)),
                  _const_spec(cs.shape), _const_spec(tf.shape), _const_spec(m4.shape),
                  pl.BlockSpec((1, 1, LANES, LANES), lambda i, c: (layer, c, 0, 0))],
        out_specs=pl.BlockSpec((1, 1, L, LANES), lambda i, c: (c, i, 0, 0)),
        out_shape=jax.ShapeDtypeStruct((ct, b, L, LANES), F32),
        scratch_shapes=[pltpu.VMEM((L, LANES), F32), pltpu.VMEM((L, LANES), F32),
                        pltpu.VMEM((nb * pitch, LANES), F32)],
        compiler_params=_params(("parallel", "parallel")),
        name="fnet_mixer",
    )(pf, cs, tf, m4, wg)


def _inproj_kernel(xm_ref, xp_ref, xn_ref, mod_ref, g_ref, w_ref, sw_ref, sb_ref,
                   u_ref, pf_ref, *, tm, dconv):
    i = pl.program_id(1)
    shift, scale = mod_ref[0, 0, 0:1, :], mod_ref[0, 0, 1:2, :]
    h = _halo_rows(xm_ref, xp_ref, xn_ref, g_ref[0], scale, shift,
                   i == 0, i == pl.num_programs(1) - 1)
    p = jnp.dot(h, w_ref[0], preferred_element_type=F32)
    u = _dwconv3(p[:, :dconv], sw_ref.at[0], sb_ref.at[0], tm)
    pf = p[HALO:HALO + tm, dconv:]
    _scatter_slab_order(u_ref, u, CONV_N2)
    _scatter_slab_order(pf_ref, pf, FNET_NB)


def _scatter_slab_order(o_ref, v, minor):
    for n1 in range(v.shape[0] // minor):
        for j in range(v.shape[1] // LANES):
            o_ref[j, 0, :, n1, :] = v[n1 * minor:(n1 + 1) * minor, j * LANES:(j + 1) * LANES]


def _gather_slab_order(dst_ref, src_ref, minor):
    for n1 in range(src_ref.shape[3]):
        for j in range(src_ref.shape[0]):
            dst_ref[n1 * minor:(n1 + 1) * minor, j * LANES:(j + 1) * LANES] = (
                src_ref[j, 0, :, n1, :].astype(dst_ref.dtype))


def _row_halo_specs(tm, L, d):
    nblk = L // HALO
    step = tm // HALO
    return [pl.BlockSpec((1, tm, d), lambda b, i: (b, i, 0)),
            pl.BlockSpec((1, HALO, d), lambda b, i: (b, jnp.maximum(i * step - 1, 0), 0)),
            pl.BlockSpec((1, HALO, d), lambda b, i: (b, jnp.minimum((i + 1) * step, nblk - 1), 0))]


def _in_projection(x, mod, g_pre, w_in, short_w, short_b, layer, tm):
    b, L, d = x.shape
    e = w_in.shape[-1]
    dconv = short_w.shape[-1]
    return pl.pallas_call(
        functools.partial(_inproj_kernel, tm=tm, dconv=dconv),
        grid=(b, L // tm),
        in_specs=_row_halo_specs(tm, L, d) + [
            pl.BlockSpec((1, 1, N_MOD, d), lambda b_, i: (layer, b_, 0, 0)),
            pl.BlockSpec((1, 1, d), lambda b_, i: (layer, 0, 0)),
            pl.BlockSpec((1, d, e), lambda b_, i: (layer, 0, 0), pipeline_mode=pl.Buffered(1)),
            pl.BlockSpec((1, 3, dconv), lambda b_, i: (layer, 0, 0)),
            pl.BlockSpec((1, 1, dconv), lambda b_, i: (layer, 0, 0))],
        out_specs=[pl.BlockSpec((dconv // LANES, 1, CONV_N2, tm // CONV_N2, LANES),
                                lambda b_, i: (0, b_, 0, i, 0)),
                   pl.BlockSpec(((e - dconv) // LANES, 1, FNET_NB, tm // FNET_NB, LANES),
                                lambda b_, i: (0, b_, 0, i, 0))],
        out_shape=[jax.ShapeDtypeStruct((dconv // LANES, b, CONV_N2, L // CONV_N2, LANES), F32),
                   jax.ShapeDtypeStruct(((e - dconv) // LANES, b, FNET_NB, L // FNET_NB, LANES),
                                        F32)],
        compiler_params=_params(("parallel", "parallel")),
        name="in_projection",
    )(x, x, x, mod, g_pre, w_in, short_w, short_b)


def _outproj_kernel(x_ref, yh_ref, yf_ref, mod_ref, g_ref, w_ref, o_ref, yc_ref, *, dh):
    _gather_slab_order(yc_ref.at[:, pl.ds(0, dh)], yh_ref, CONV_N2)
    _gather_slab_order(yc_ref.at[:, pl.ds(dh, yc_ref.shape[1] - dh)], yf_ref, yf_ref.shape[2])
    y = jnp.dot(yc_ref[...], w_ref[0], preferred_element_type=F32)
    gate = mod_ref[0, 0, 2:3, :]
    o_ref[0] = x_ref[0] + gate * _rmsnorm(y, g_ref[0])


def _out_projection(x, yh, yf, mod, g_post, w_out, layer, tm):
    b, L, d = x.shape
    dh = yh.shape[0] * LANES
    df = yf.shape[0] * LANES
    return pl.pallas_call(
        functools.partial(_outproj_kernel, dh=dh),
        grid=(b, L // tm),
        in_specs=[pl.BlockSpec((1, tm, d), lambda b_, i: (b_, i, 0)),
                  pl.BlockSpec((yh.shape[0], 1, CONV_N2, tm // CONV_N2, LANES),
                               lambda b_, i: (0, b_, 0, i, 0)),
                  pl.BlockSpec((yf.shape[0], 1, yf.shape[2], tm // yf.shape[2], LANES),
                               lambda b_, i: (0, b_, 0, i, 0)),
                  pl.BlockSpec((1, 1, N_MOD, d), lambda b_, i: (layer, b_, 0, 0)),
                  pl.BlockSpec((1, 1, d), lambda b_, i: (layer, 0, 0)),
                  pl.BlockSpec((1, dh + df, d), lambda b_, i: (layer, 0, 0),
                               pipeline_mode=pl.Buffered(1))],
        out_specs=pl.BlockSpec((1, tm, d), lambda b_, i: (b_, i, 0)),
        out_shape=jax.ShapeDtypeStruct((b, L, d), F32),
        scratch_shapes=[pltpu.VMEM((tm, dh + df), BF16)],
        compiler_params=_params(("parallel", "parallel")),
        name="out_projection",
    )(x, yh, yf, mod, g_post, w_out)


def _ffn_kernel(xm_ref, xp_ref, xn_ref, mod_ref, gpre_ref, gpost_ref, wu_ref, cw_ref, cb_ref,
                wd_ref, o_ref, h_ref, *, tm, tf, nf):
    i = pl.program_id(1)
    shift, scale = mod_ref[0, 0, 3:4, :], mod_ref[0, 0, 4:5, :]
    h_ref[...] = _halo_rows(xm_ref, xp_ref, xn_ref, gpre_ref[0], scale, shift,
                            i == 0, i == pl.num_programs(1) - 1)
    dff = nf * tf

    def up(f):
        h = h_ref[...]
        return [jnp.dot(h, wu_ref[0, :, pl.ds(c0, tf)], preferred_element_type=F32)
                for c0 in (f * tf, dff + f * tf)]

    def gate_act(f, pa, pb):
        a = _dwconv3(pa, cw_ref.at[0, :, pl.ds(f * tf, tf)], cb_ref.at[0, :, pl.ds(f * tf, tf)], tm)
        b = _dwconv3(pb, cw_ref.at[0, :, pl.ds(dff + f * tf, tf)],
                     cb_ref.at[0, :, pl.ds(dff + f * tf, tf)], tm)
        return (0.5 * a * (1.0 + lax.erf(a * (1.0 / math.sqrt(2.0)))) * b).astype(BF16)

    ups, acts, acc = {}, {}, None
    for s in range(nf + 2):
        if s < nf:
            ups[s] = up(s)
        if 0 <= s - 1 < nf:
            acts[s - 1] = gate_act(s - 1, *ups.pop(s - 1))
        if 0 <= s - 2 < nf:
            y = jnp.dot(acts.pop(s - 2), wd_ref[0, pl.ds((s - 2) * tf, tf), :],
                        preferred_element_type=F32)
            acc = y if acc is None else acc + y
    gate = mod_ref[0, 0, 5:6, :]
    o_ref[0] = xm_ref[0] + gate * _rmsnorm(acc, gpost_ref[0])


def _ffn(x, mod, g_pre, g_post, w_up, dw_w, dw_b, w_down, layer, tm, tf):
    b, L, d = x.shape
    dff = w_down.shape[1]
    idx = lambda b_, i: (layer, 0, 0)
    once = dict(pipeline_mode=pl.Buffered(1))
    return pl.pallas_call(
        functools.partial(_ffn_kernel, tm=tm, tf=tf, nf=dff // tf),
        grid=(b, L // tm),
        in_specs=_row_halo_specs(tm, L, d) + [
            pl.BlockSpec((1, 1, N_MOD, d), lambda b_, i: (layer, b_, 0, 0)),
            pl.BlockSpec((1, 1, d), idx), pl.BlockSpec((1, 1, d), idx),
            pl.BlockSpec((1, d, 2 * dff), idx, **once),
            pl.BlockSpec((1, 3, 2 * dff), idx), pl.BlockSpec((1, 1, 2 * dff), idx),
            pl.BlockSpec((1, dff, d), idx, **once)],
        out_specs=pl.BlockSpec((1, tm, d), lambda b_, i: (b_, i, 0)),
        out_shape=jax.ShapeDtypeStruct((b, L, d), F32),
        scratch_shapes=[pltpu.VMEM((tm + 2 * HALO, d), BF16)],
        compiler_params=_params(("parallel", "parallel")),
        name="geglu_ffn",
    )(x, x, x, mod, g_pre, g_post, w_up, dw_w, dw_b, w_down)


def kernel(x, c, ada_w, ada_b, g_mix_pre, g_mix_post, w_in, short_w, short_b, filt_w1, filt_b1,
           filt_w2, filt_b2, filt_freq, filt_w3, hyena_d, fnet_w, w_out, g_ffn_pre, g_ffn_post,
           w_up, dw_w, dw_b, w_down):
    b, L, d = x.shape
    depth = ada_w.shape[0]
    dh = hyena_d.shape[-1]
    df = w_in.shape[-1] - (HYENA_ORDER + 1) * dh
    cg = df // FNET_GROUPS
    assert b % 2 == 0 and L % (CONV_N2 * SUBLANES) == 0 and L % (FNET_NB * SUBLANES) == 0
    assert dh % LANES == 0 and df % LANES == 0 and LANES % cg == 0
    tm_proj = min(1024, L)
    tm_ffn = min(512, L)
    tf = 256

    row = lambda a: a.reshape(depth, 1, a.shape[-1])
    mod = _modulation(c, ada_w, ada_b).reshape(depth, b, N_MOD, d)

    w1p = jnp.pad(filt_w1, ((0, 0), (0, LANES - filt_w1.shape[1]), (0, 0)))
    kf = _filter_spectra(L, w1p, row(filt_b1), filt_w2, row(filt_b2), row(filt_freq),
                         filt_w3, dh)
    hd = hyena_d.reshape(depth, HYENA_ORDER, 1, dh)

    gpt = LANES // cg
    eye = jnp.eye(gpt, dtype=F32)
    wg = jnp.einsum("ltacd,ae->ltaced", fnet_w.reshape(depth, df // LANES, gpt, cg, cg), eye)
    wg = wg.reshape(depth, df // LANES, LANES, LANES).astype(BF16)

    w_in_b = w_in.astype(BF16)
    w_out_b = w_out.astype(BF16)
    w_up_b = w_up.astype(BF16)
    w_down_b = w_down.astype(BF16)
    g_mix_pre, g_mix_post = row(g_mix_pre), row(g_mix_post)
    g_ffn_pre, g_ffn_post = row(g_ffn_pre), row(g_ffn_post)
    short_b3, dw_b3 = row(short_b), row(dw_b)
    nct = dh // LANES

    for l in range(depth):
        u, pf = _in_projection(x, mod, g_mix_pre, w_in_b, short_w, short_b3, l, tm_proj)
        u5 = u.reshape(u.shape[0], 2, b // 2, L, LANES)
        z2 = _long_conv(u5, 0, u5, nct, kf, hd, l, 0)
        yh = _long_conv(z2, 0, u5, 2 * nct, kf, hd, l, 1)
        yf = _fnet_mixer(pf.reshape(pf.shape[0], b, L, LANES), wg, l)
        x = _out_projection(x, yh.reshape(nct, b, CONV_N2, L // CONV_N2, LANES),
                            yf.reshape(yf.shape[0], b, L // FNET_NB, FNET_NB, LANES),
                            mod, g_mix_post, w_out_b, l, tm_proj)
        x = _ffn(x, mod, g_ffn_pre, g_ffn_post, w_up_b, dw_w, dw_b3, w_down_b, l, tm_ffn, tf)
    return x
```

```python
import functools
import math

import numpy as np
import jax
import jax.numpy as jnp
from jax import lax
from jax.experimental import pallas as pl
from jax.experimental.pallas import tpu as pltpu

F32 = jnp.float32
BF16 = jnp.bfloat16

LANES = 128
SUBLANES = 8
HALO = SUBLANES
VMEM_LIMIT = 56 * 1024 * 1024

FNET_GROUPS = 8
HYENA_ORDER = 2
N_DIRS = 2
FILTER_BANDS = 16
FILTER_HIDDEN = 64
DECAY_TARGET = 1e-2
FAST_DECAY_PCT = 0.3
SLOW_DECAY_PCT = 1.5
N_MOD = 6
NORM_EPS = 1e-6
FILTER_EPS = 1e-6

CONV_N2 = 128
BLOCK_ROWS = CONV_N2 * SUBLANES
SLAB_PAD = 8
SUB_ROWS = 256
UNROLL = 8


def _params(sem, vmem=VMEM_LIMIT):
    return pltpu.CompilerParams(dimension_semantics=sem, vmem_limit_bytes=vmem)


def _const_spec(shape):
    nd = len(shape)
    return pl.BlockSpec(shape, lambda *_: (0,) * nd, pipeline_mode=pl.Buffered(1))


@functools.lru_cache(maxsize=None)
def _conv_tables(L):
    n2n = CONV_N2
    nh = L // n2n
    n1n = 2 * nh
    N = 2 * L
    k1 = np.arange(n1n)[:, None]
    n1 = np.arange(nh)[None, :]
    tf = np.zeros((n2n, 2 * n1n, 2 * nh), np.float64)
    for n2 in range(n2n):
        th = 2.0 * np.pi * ((k1 * (n2n * n1 + n2)) % N) / N
        c, s = np.cos(th), np.sin(th)
        tf[n2] = np.block([[c, s], [-s, c]])
    ti = np.transpose(tf, (0, 2, 1))
    k2 = np.arange(n2n)[:, None]
    m = np.arange(n2n)[None, :]
    th2 = 2.0 * np.pi * ((k2 * m) % n2n) / n2n
    c2, s2 = np.cos(th2), np.sin(th2)
    m2 = np.block([[c2, s2], [-s2, c2]])
    return (jnp.asarray(tf.reshape(n2n * 2 * n1n, 2 * nh), BF16),
            jnp.asarray(ti.reshape(n2n * 2 * nh, 2 * n1n), BF16),
            jnp.asarray(m2, BF16), jnp.asarray(m2.T, BF16))


@functools.lru_cache(maxsize=None)
def _fnet_tables(L, cg):
    n2n = CONV_N2
    n1n = L // n2n
    q = n2n // n1n
    scale = 1.0 / math.sqrt(L * cg)
    j = np.arange(cg)
    thc = 2.0 * np.pi * ((j[:, None] * j[None, :]) % cg) / cg
    reps = LANES // cg
    cc = np.kron(np.eye(reps), np.cos(thc)) * scale
    sc = np.kron(np.eye(reps), np.sin(thc)) * scale
    cs = np.concatenate([cc, -sc], axis=1)
    k1 = np.arange(n1n)[:, None]
    n1 = np.arange(n1n)[None, :]
    tf = np.zeros((n2n, 2 * n1n, 2 * n1n), np.float64)
    for n2 in range(n2n):
        th = 2.0 * np.pi * ((k1 * (n2n * n1 + n2)) % L) / L
        c, s = np.cos(th), np.sin(th)
        tf[n2] = np.block([[c, s], [-s, c]])
    k2 = np.array([q * a + r for r in range(q) for a in range(n1n)])[:, None]
    m = np.arange(n2n)[None, :]
    th2 = 2.0 * np.pi * ((k2 * m) % n2n) / n2n
    m4 = np.concatenate([np.cos(th2), np.sin(th2)], axis=1)
    return (jnp.asarray(cs, BF16), jnp.asarray(tf.reshape(n2n * 2 * n1n, 2 * n1n), BF16),
            jnp.asarray(m4, BF16))


@functools.lru_cache(maxsize=None)
def _filter_features(L):
    pos = np.arange(L, dtype=np.float32)
    bands = np.linspace(1e-4, FILTER_BANDS - 1, FILTER_BANDS, dtype=np.float32)
    t = pos / np.float32(max(L - 1, 1))
    ang = np.float32(2.0 * math.pi / L) * pos[:, None] * bands[None, :]
    z = np.concatenate([t[:, None], np.cos(ang), -np.sin(ang)], axis=-1).astype(np.float32)
    zp = np.zeros((L, LANES), np.float32)
    zp[:, :z.shape[1]] = z
    return jnp.asarray(zp)


def _decay_rates(dh):
    min_decay = math.log(DECAY_TARGET) / SLOW_DECAY_PCT
    max_decay = math.log(DECAY_TARGET) / FAST_DECAY_PCT
    return jnp.abs(jnp.linspace(min_decay, max_decay, dh, dtype=F32)).reshape(1, dh)


def _loop(n, body):
    for i in range(n):
        body(i, 0)


def _aligned(i, m):
    return i if isinstance(i, int) else pl.multiple_of(i, m)


def _dot3(a, b):
    a_hi = a.astype(BF16)
    b_hi = b.astype(BF16)
    a_lo = (a - a_hi.astype(F32)).astype(BF16)
    b_lo = (b - b_hi.astype(F32)).astype(BF16)
    dot = functools.partial(jnp.dot, preferred_element_type=F32)
    return dot(a_hi, b_hi) + (dot(a_hi, b_lo) + dot(a_lo, b_hi))


def _modnorm(x, g, scale, shift):
    ms = jnp.mean(x * x, axis=-1, keepdims=True)
    return (x * lax.rsqrt(ms + NORM_EPS) * g) * (1.0 + scale) + shift


def _rmsnorm(y, g):
    ms = jnp.mean(y * y, axis=-1, keepdims=True)
    return y * lax.rsqrt(ms + NORM_EPS) * g


def _lanes(ref, rows=slice(None)):
    return jnp.concatenate([ref[j, 0, rows, :] for j in range(ref.shape[0])], axis=1)


def _store_lanes(ref, j0, v, rows=slice(None)):
    for j in range(v.shape[1] // LANES):
        ref[j0 + j, 0, rows, :] = v[:, j * LANES:(j + 1) * LANES]


def _halo_rows(xm_ref, xp_ref, xn_ref, g, scale, shift, first, last):
    hm = _modnorm(_lanes(xm_ref), g, scale, shift)
    hp = _modnorm(_lanes(xp_ref), g, scale, shift) * jnp.where(first, 0.0, 1.0)
    hn = _modnorm(_lanes(xn_ref), g, scale, shift) * jnp.where(last, 0.0, 1.0)
    return jnp.concatenate([hp, hm, hn], axis=0).astype(BF16)


def _dwconv3(p, w_ref, b_ref):
    rows = p.shape[0] - 2 * HALO
    cur = p[HALO:HALO + rows]
    sub = lax.broadcasted_iota(jnp.int32, (SUBLANES, p.shape[1]), 0)
    wrap_prev = jnp.where(sub == 0, pltpu.roll(p[:HALO], 1, 0),
                          pltpu.roll(cur[rows - SUBLANES:], 1, 0))
    wrap_next = jnp.where(sub == SUBLANES - 1, pltpu.roll(p[HALO + rows:], SUBLANES - 1, 0),
                          pltpu.roll(cur[:SUBLANES], SUBLANES - 1, 0))
    prev = jnp.concatenate([wrap_prev, cur[:rows - SUBLANES]], axis=0)
    nxt = jnp.concatenate([cur[SUBLANES:], wrap_next], axis=0)
    return b_ref[...] + prev * w_ref[0:1, :] + cur * w_ref[1:2, :] + nxt * w_ref[2:3, :]


def _slab(ref, idx, n2, nblk):
    return [ref[idx + (pl.ds(i * BLOCK_ROWS + n2 * SUBLANES, SUBLANES), slice(None))]
            for i in range(nblk)]


def _mod_kernel(c_ref, w_ref, b_ref, o_ref):
    c = c_ref[...]
    act = c * jax.nn.sigmoid(c)
    o_ref[0] = jnp.dot(act, w_ref[0], preferred_element_type=F32,
                       precision=lax.Precision.HIGHEST) + b_ref[0]


def _modulation(c, ada_w, ada_b):
    depth, d, nm = ada_w.shape
    b = c.shape[0]
    tn = nm // 4
    return pl.pallas_call(
        _mod_kernel,
        grid=(depth, nm // tn),
        in_specs=[pl.BlockSpec((b, d), lambda l, j: (0, 0)),
                  pl.BlockSpec((1, d, tn), lambda l, j: (l, 0, j)),
                  pl.BlockSpec((1, 1, tn), lambda l, j: (l, 0, j))],
        out_specs=pl.BlockSpec((1, b, tn), lambda l, j: (l, 0, j)),
        out_shape=jax.ShapeDtypeStruct((depth, b, nm), F32),
        compiler_params=_params(("parallel", "parallel")),
        name="adaln_modulation",
    )(c, ada_w, ada_b.reshape(depth, 1, nm))


def _stage1(load_x, t_ref, s1_ref, n_iter, rows_out, pitch):
    def body(n2, carry):
        t = t_ref[pl.ds(_aligned(n2 * rows_out, rows_out), rows_out), :]
        a = jnp.dot(t, load_x(n2), preferred_element_type=F32)
        s1_ref[pl.ds(_aligned(n2 * pitch, SUBLANES), rows_out), :] = a
        return carry
    _loop(n_iter, body)


def _load_slab_column(s1_ref, k1, n1n, n2n, pitch):
    ar = s1_ref[pl.ds(k1, n2n, stride=pitch), :]
    ai = s1_ref[pl.ds(n1n + k1, n2n, stride=pitch), :]
    return jnp.concatenate([ar, ai], axis=0).astype(BF16)


def _filter_kernel(zf_ref, w1_ref, b1_ref, w2_ref, b2_ref, fr_ref, w3f_ref, w3b_ref, dl_ref,
                   tf_ref, m2_ref, kf_ref, h_ref, fwd_ref, bwd_ref, s1_ref, *, L):
    n2n = CONV_N2
    nh = L // n2n
    n1n = 2 * nh
    pitch = 2 * n1n + SLAB_PAD

    @pl.when(pl.program_id(1) == 0)
    def _():
        fr = fr_ref[0]
        h = jnp.sin(fr * (_dot3(zf_ref[...], w1_ref[0]) + b1_ref[0]))
        h_ref[...] = jnp.sin(fr * (_dot3(h, w2_ref[0]) + b2_ref[0]))

    h = h_ref[...]
    hf = _dot3(h, w3f_ref[0])
    hb = _dot3(h, w3b_ref[0])
    pos = lax.broadcasted_iota(jnp.int32, (L, LANES), 0)
    t = pos.astype(F32) / float(max(L - 1, 1))
    decay = jnp.exp(-t * dl_ref[...])
    hf = hf * decay
    hb = jnp.where(pos == 0, 0.0, hb * decay)
    norm = (jnp.sum(jnp.abs(hf), axis=0, keepdims=True)
            + jnp.sum(jnp.abs(hb), axis=0, keepdims=True) + FILTER_EPS)
    inv = 1.0 / (norm * float(2 * L))
    fwd_ref[...] = hf * inv
    bwd_ref[...] = hb * inv

    zeros = jnp.zeros((nh, LANES), F32)
    for src_ref, conj in ((fwd_ref, False), (bwd_ref, True)):
        def load_x(n2, src_ref=src_ref):
            xr = src_ref[pl.ds(n2, nh, stride=n2n), :]
            return jnp.concatenate([xr, zeros], axis=0).astype(BF16)
        _stage1(load_x, tf_ref, s1_ref, n2n, 2 * n1n, pitch)

        def body(k1, carry, conj=conj):
            a = _load_slab_column(s1_ref, k1, n1n, n2n, pitch)
            u = jnp.dot(m2_ref[...], a, preferred_element_type=F32)
            base = _aligned(k1 * 2 * n2n, 2 * n2n)
            if not conj:
                kf_ref[0, 0, pl.ds(base, 2 * n2n), :] = u
            else:
                kf_ref[0, 0, pl.ds(base, n2n), :] += u[:n2n]
                kf_ref[0, 0, pl.ds(base + n2n, n2n), :] -= u[n2n:]
            return carry
        _loop(n1n, body)


def _filter_spectra(L, w1p, b1, w2, b2, freq, w3, dh):
    depth = w1p.shape[0]
    n2n = CONV_N2
    nh = L // n2n
    n1n = 2 * nh
    pitch = 2 * n1n + SLAB_PAD
    tf, _, m2, _ = _conv_tables(L)
    zf = _filter_features(L)
    ct = dh // LANES
    fh = FILTER_HIDDEN
    cols = N_DIRS * ct

    lyr = lambda l, i: (l, 0, 0)
    return pl.pallas_call(
        functools.partial(_filter_kernel, L=L),
        grid=(depth, HYENA_ORDER * ct),
        in_specs=[_const_spec(zf.shape),
                  pl.BlockSpec((1, LANES, fh), lyr), pl.BlockSpec((1, 1, fh), lyr),
                  pl.BlockSpec((1, fh, fh), lyr), pl.BlockSpec((1, 1, fh), lyr),
                  pl.BlockSpec((1, 1, fh), lyr),
                  pl.BlockSpec((1, fh, LANES), lambda l, i: (l, 0, (i // ct) * cols + i % ct)),
                  pl.BlockSpec((1, fh, LANES),
                               lambda l, i: (l, 0, (i // ct) * cols + ct + i % ct)),
                  pl.BlockSpec((1, LANES), lambda l, i: (0, i % ct)),
                  _const_spec(tf.shape), _const_spec(m2.shape)],
        out_specs=pl.BlockSpec((1, 1, n1n * 2 * n2n, LANES),
                               lambda l, i: (l, i // ct, 0, i % ct)),
        out_shape=jax.ShapeDtypeStruct((depth, HYENA_ORDER, n1n * 2 * n2n, dh), F32),
        scratch_shapes=[pltpu.VMEM((L, fh), F32),
                        pltpu.VMEM((L, LANES), F32), pltpu.VMEM((L, LANES), F32),
                        pltpu.VMEM((n2n * pitch, LANES), F32)],
        compiler_params=_params(("parallel", "arbitrary")),
        name="hyena_filter_spectrum",
    )(zf, w1p, b1, w2, b2, freq, w3, w3, _decay_rates(dh), tf, m2)


def _conv_kernel(z_ref, g_ref, kf_ref, d_ref, tf_ref, ti_ref, m2_ref, m2i_ref, o_ref, s1_ref,
                 *, L):
    n2n = CONV_N2
    nh = L // n2n
    n1n = 2 * nh
    pitch = 2 * n1n + SLAB_PAD
    nblk = L // BLOCK_ROWS

    def load_x(n2):
        return jnp.concatenate(_slab(z_ref, (0, 0, 0), n2, nblk)
                               + _slab(z_ref, (0, 1, 0), n2, nblk), axis=0).astype(BF16)
    _stage1(load_x, tf_ref, s1_ref, n2n, 2 * n1n, pitch)

    def mid(j, carry):
        cols = [j * UNROLL + i for i in range(UNROLL)]
        a = [_load_slab_column(s1_ref, k1, n1n, n2n, pitch) for k1 in cols]
        w = []
        for i in range(0, UNROLL, 2):
            u = jnp.dot(m2_ref[...], jnp.concatenate(a[i:i + 2], axis=1),
                        preferred_element_type=F32)
            v = []
            for h in range(2):
                ur, ui = u[:n2n, h * LANES:(h + 1) * LANES], u[n2n:, h * LANES:(h + 1) * LANES]
                base = _aligned(cols[i + h] * 2 * n2n, 2 * n2n)
                kr = kf_ref[0, 0, pl.ds(base, n2n), :]
                ki = kf_ref[0, 0, pl.ds(base + n2n, n2n), :]
                v.append(jnp.concatenate([ur * kr - ui * ki, ur * ki + ui * kr], axis=0))
            wk = jnp.dot(m2i_ref[...], jnp.concatenate(v, axis=1).astype(BF16),
                         preferred_element_type=F32)
            w += [wk[:, :LANES], wk[:, LANES:]]
        for k1, wk in zip(cols, w):
            s1_ref[pl.ds(k1, n2n, stride=pitch), :] = wk[:n2n]
            s1_ref[pl.ds(n1n + k1, n2n, stride=pitch), :] = wk[n2n:]
        return carry
    _loop(n1n // UNROLL, mid)

    d = d_ref[0, 0]

    def last(n2, carry):
        w = s1_ref[pl.ds(_aligned(n2 * pitch, SUBLANES), 2 * n1n), :].astype(BF16)
        t = ti_ref[pl.ds(_aligned(n2 * 2 * nh, 2 * nh), 2 * nh), :]
        y = jnp.dot(t, w, preferred_element_type=F32)
        for half in range(2):
            zz = _slab(z_ref, (0, half, 0), n2, nblk)
            gg = _slab(g_ref, (0, half, 0), n2, nblk)
            for i in range(nblk):
                r0 = half * nh + i * SUBLANES
                o_ref[0, half, 0, pl.ds(i * BLOCK_ROWS + n2 * SUBLANES, SUBLANES), :] = (
                    gg[i] * (y[r0:r0 + SUBLANES] + d * zz[i]))
        return carry
    _loop(n2n, last)


def _long_conv(z, z_col0, gate, gate_col0, kf, d, layer, order):
    _, _, bh, L, _ = z.shape
    dh = kf.shape[-1]
    ct = dh // LANES
    n2n = CONV_N2
    nh = L // n2n
    n1n = 2 * nh
    pitch = 2 * n1n + SLAB_PAD
    tf, ti, m2, m2i = _conv_tables(L)
    blk = (1, 2, 1, L, LANES)
    return pl.pallas_call(
        functools.partial(_conv_kernel, L=L),
        grid=(ct, bh),
        in_specs=[pl.BlockSpec(blk, lambda c, p: (z_col0 + c, 0, p, 0, 0)),
                  pl.BlockSpec(blk, lambda c, p: (gate_col0 + c, 0, p, 0, 0)),
                  pl.BlockSpec((1, 1, n1n * 2 * n2n, LANES), lambda c, p: (layer, order, 0, c),
                               pipeline_mode=pl.Buffered(1)),
                  pl.BlockSpec((1, 1, 1, LANES), lambda c, p: (layer, order, 0, c)),
                  _const_spec(tf.shape), _const_spec(ti.shape),
                  _const_spec(m2.shape), _const_spec(m2i.shape)],
        out_specs=pl.BlockSpec(blk, lambda c, p: (c, 0, p, 0, 0)),
        out_shape=jax.ShapeDtypeStruct((ct, 2, bh, L, LANES), F32),
        scratch_shapes=[pltpu.VMEM((n2n * pitch, LANES), F32)],
        compiler_params=_params(("parallel", "parallel")),
        name="hyena_long_conv",
    )(z, gate, kf, d, tf, ti, m2, m2i)


def _fnet_kernel(p_ref, cs_ref, tf_ref, m4_ref, wg_ref, o_ref, a_ref, b_ref, s1_ref, *, L):
    n2n = CONV_N2
    n1n = L // n2n
    pitch = 2 * n1n + SLAB_PAD
    nblk = L // BLOCK_ROWS
    wg = wg_ref[0, 0]
    csw = jnp.concatenate(
        [jnp.dot(cs_ref[:, h * LANES:(h + 1) * LANES], wg, preferred_element_type=F32)
         for h in range(2)], axis=1).astype(BF16)
    ab = jnp.dot(p_ref[0, 0].astype(BF16), csw, preferred_element_type=F32)
    a_ref[...] = ab[:, :LANES]
    b_ref[...] = ab[:, LANES:]

    def load_x(n2):
        return jnp.concatenate(_slab(a_ref, (), n2, nblk) + _slab(b_ref, (), n2, nblk),
                               axis=0).astype(BF16)
    _stage1(load_x, tf_ref, s1_ref, n2n, 2 * n1n, pitch)

    def body(kp, carry):
        a = jnp.concatenate([_load_slab_column(s1_ref, 2 * kp + h, n1n, n2n, pitch)
                             for h in range(2)], axis=1)
        y = jnp.dot(m4_ref[...], a, preferred_element_type=F32)
        for h in range(2):
            k1 = 2 * kp + h
            for rr in range(n2n // n1n):
                for i in range(nblk):
                    src = rr * n1n + i * SUBLANES
                    dst = i * BLOCK_ROWS + (k1 + n1n * rr) * SUBLANES
                    o_ref[0, 0, pl.ds(dst, SUBLANES), :] = y[src:src + SUBLANES,
                                                             h * LANES:(h + 1) * LANES]
        return carry
    _loop(n1n // 2, body)


def _fnet_mixer(pf, wg, layer):
    ct, b, L, _ = pf.shape
    cg = ct * LANES // FNET_GROUPS
    cs, tf, m4 = _fnet_tables(L, cg)
    n2n = CONV_N2
    pitch = 2 * (L // n2n) + SLAB_PAD
    return pl.pallas_call(
        functools.partial(_fnet_kernel, L=L),
        grid=(b, ct),
        in_specs=[pl.BlockSpec((1, 1, L, LANES), lambda i, c: (c, i, 0, 0)),
                  _const_spec(cs.shape), _const_spec(tf.shape), _const_spec(m4.shape),
                  pl.BlockSpec((1, 1, LANES, LANES), lambda i, c: (layer, c, 0, 0))],
        out_specs=pl.BlockSpec((1, 1, L, LANES), lambda i, c: (c, i, 0, 0)),
        out_shape=jax.ShapeDtypeStruct((ct, b, L, LANES), F32),
        scratch_shapes=[pltpu.VMEM((L, LANES), F32), pltpu.VMEM((L, LANES), F32),
                        pltpu.VMEM((n2n * pitch, LANES), F32)],
        compiler_params=_params(("parallel", "parallel")),
        name="fnet_mixer",
    )(pf, cs, tf, m4, wg)


def _to_block_slab_kernel(x_ref, o_ref):
    for j in range(o_ref.shape[0]):
        flat = o_ref.at[j, 0]
        for r in range(SUBLANES):
            flat[pl.ds(r, CONV_N2, stride=SUBLANES), :] = (
                x_ref[0, r * CONV_N2:(r + 1) * CONV_N2, j * LANES:(j + 1) * LANES])


def _from_block_slab_kernel(x_ref, o_ref):
    for j in range(x_ref.shape[0]):
        flat = x_ref.at[j, 0]
        for r in range(SUBLANES):
            o_ref[0, r * CONV_N2:(r + 1) * CONV_N2, j * LANES:(j + 1) * LANES] = (
                flat[pl.ds(r, CONV_N2, stride=SUBLANES), :])


def _to_block_slab(x):
    b, L, d = x.shape
    return pl.pallas_call(
        _to_block_slab_kernel,
        grid=(b, L // BLOCK_ROWS),
        in_specs=[pl.BlockSpec((1, BLOCK_ROWS, d), lambda b_, i: (b_, i, 0))],
        out_specs=pl.BlockSpec((d // LANES, 1, BLOCK_ROWS, LANES), lambda b_, i: (0, b_, i, 0)),
        out_shape=jax.ShapeDtypeStruct((d // LANES, b, L, LANES), F32),
        compiler_params=_params(("parallel", "parallel")),
        name="to_block_slab_order",
    )(x)


def _from_block_slab(xs):
    nt, b, L, _ = xs.shape
    return pl.pallas_call(
        _from_block_slab_kernel,
        grid=(b, L // BLOCK_ROWS),
        in_specs=[pl.BlockSpec((nt, 1, BLOCK_ROWS, LANES), lambda b_, i: (0, b_, i, 0))],
        out_specs=pl.BlockSpec((1, BLOCK_ROWS, nt * LANES), lambda b_, i: (b_, i, 0)),
        out_shape=jax.ShapeDtypeStruct((b, L, nt * LANES), F32),
        compiler_params=_params(("parallel", "parallel")),
        name="from_block_slab_order",
    )(xs)


def _block_halo_specs(nt, L):
    nslab = L // SUBLANES
    step = BLOCK_ROWS // SUBLANES
    return [pl.BlockSpec((nt, 1, BLOCK_ROWS, LANES), lambda b, i: (0, b, i, 0)),
            pl.BlockSpec((nt, 1, SUBLANES, LANES),
                         lambda b, i: (0, b, jnp.maximum(i * step - 1, 0), 0)),
            pl.BlockSpec((nt, 1, SUBLANES, LANES),
                         lambda b, i: (0, b, jnp.minimum((i + 1) * step, nslab - 1), 0))]


def _inproj_kernel(xm_ref, xp_ref, xn_ref, mod_ref, g_ref, w_ref, sw_ref, sb_ref,
                   u_ref, pf_ref, h_ref, *, tcol):
    i = pl.program_id(1)
    shift, scale = mod_ref[0, 0, 0:1, :], mod_ref[0, 0, 1:2, :]
    h_ref[...] = _halo_rows(xm_ref, xp_ref, xn_ref, g_ref[0], scale, shift,
                            i == 0, i == pl.num_programs(1) - 1)
    nconv = u_ref.shape[0] * LANES // tcol
    ncol = nconv + pf_ref.shape[0] * LANES // tcol
    ps = {}
    for s in range(ncol + 1):
        if s < ncol:
            ps[s] = jnp.dot(h_ref[...], w_ref[0, :, pl.ds(s * tcol, tcol)],
                            preferred_element_type=F32)
        if s >= 1:
            f = s - 1
            p = ps.pop(f)
            if f < nconv:
                cols = pl.ds(f * tcol, tcol)
                _store_lanes(u_ref, f * tcol // LANES,
                             _dwconv3(p, sw_ref.at[0, :, cols], sb_ref.at[0, :, cols]))
            else:
                _store_lanes(pf_ref, (f - nconv) * tcol // LANES, p[HALO:HALO + BLOCK_ROWS])


def _in_projection(xs, mod, g_pre, w_in, short_w, short_b, layer):
    nt, b, L, _ = xs.shape
    d = nt * LANES
    e = w_in.shape[-1]
    dconv = short_w.shape[-1]
    out_blk = lambda n: pl.BlockSpec((n, 1, BLOCK_ROWS, LANES), lambda b_, i: (0, b_, i, 0))
    return pl.pallas_call(
        functools.partial(_inproj_kernel, tcol=2 * LANES),
        grid=(b, L // BLOCK_ROWS),
        in_specs=_block_halo_specs(nt, L) + [
            pl.BlockSpec((1, 1, N_MOD, d), lambda b_, i: (layer, b_, 0, 0)),
            pl.BlockSpec((1, 1, d), lambda b_, i: (layer, 0, 0)),
            pl.BlockSpec((1, d, e), lambda b_, i: (layer, 0, 0), pipeline_mode=pl.Buffered(1)),
            pl.BlockSpec((1, 3, dconv), lambda b_, i: (layer, 0, 0)),
            pl.BlockSpec((1, 1, dconv), lambda b_, i: (layer, 0, 0))],
        out_specs=[out_blk(dconv // LANES), out_blk((e - dconv) // LANES)],
        out_shape=[jax.ShapeDtypeStruct((dconv // LANES, b, L, LANES), F32),
                   jax.ShapeDtypeStruct(((e - dconv) // LANES, b, L, LANES), F32)],
        scratch_shapes=[pltpu.VMEM((BLOCK_ROWS + 2 * HALO, d), BF16)],
        compiler_params=_params(("parallel", "parallel")),
        name="in_projection",
    )(xs, xs, xs, mod, g_pre, w_in, short_w, short_b)


def _outproj_kernel(x_ref, yh_ref, yf_ref, mod_ref, g_ref, w_ref, o_ref):
    gate = mod_ref[0, 0, 2:3, :]
    g = g_ref[0]
    ycs, ys = {}, {}
    nsub = BLOCK_ROWS // SUB_ROWS
    for s in range(nsub + 2):
        if s < nsub:
            rows = pl.ds(s * SUB_ROWS, SUB_ROWS)
            ycs[s] = jnp.concatenate([_lanes(yh_ref, rows), _lanes(yf_ref, rows)],
                                     axis=1).astype(BF16)
        if 0 <= s - 1 < nsub:
            ys[s - 1] = jnp.dot(ycs.pop(s - 1), w_ref[0], preferred_element_type=F32)
        if 0 <= s - 2 < nsub:
            rows = pl.ds((s - 2) * SUB_ROWS, SUB_ROWS)
            _store_lanes(o_ref, 0, _lanes(x_ref, rows) + gate * _rmsnorm(ys.pop(s - 2), g), rows)


def _out_projection(xs, yh, yf, mod, g_post, w_out, layer):
    nt, b, L, _ = xs.shape
    d = nt * LANES
    blk = lambda n: pl.BlockSpec((n, 1, BLOCK_ROWS, LANES), lambda b_, i: (0, b_, i, 0))
    return pl.pallas_call(
        _outproj_kernel,
        grid=(b, L // BLOCK_ROWS),
        in_specs=[blk(nt), blk(yh.shape[0]), blk(yf.shape[0]),
                  pl.BlockSpec((1, 1, N_MOD, d), lambda b_, i: (layer, b_, 0, 0)),
                  pl.BlockSpec((1, 1, d), lambda b_, i: (layer, 0, 0)),
                  pl.BlockSpec((1, (yh.shape[0] + yf.shape[0]) * LANES, d),
                               lambda b_, i: (layer, 0, 0), pipeline_mode=pl.Buffered(1))],
        out_specs=blk(nt),
        out_shape=jax.ShapeDtypeStruct(xs.shape, F32),
        compiler_params=_params(("parallel", "parallel")),
        name="out_projection",
    )(xs, yh, yf, mod, g_post, w_out)


def _ffn_kernel(xm_ref, xp_ref, xn_ref, mod_ref, gpre_ref, gpost_ref, wu_ref, cw_ref, cb_ref,
                wd_ref, o_ref, h_ref, *, tf, nf):
    i = pl.program_id(1)
    shift, scale = mod_ref[0, 0, 3:4, :], mod_ref[0, 0, 4:5, :]
    h_ref[...] = _halo_rows(xm_ref, xp_ref, xn_ref, gpre_ref[0], scale, shift,
                            i == 0, i == pl.num_programs(1) - 1)
    dff = nf * tf

    def up(f):
        h = h_ref[...]
        return [jnp.dot(h, wu_ref[0, :, pl.ds(c0, tf)], preferred_element_type=F32)
                for c0 in (f * tf, dff + f * tf)]

    def gate_act(f, pa, pb):
        a = _dwconv3(pa, cw_ref.at[0, :, pl.ds(f * tf, tf)], cb_ref.at[0, :, pl.ds(f * tf, tf)])
        b = _dwconv3(pb, cw_ref.at[0, :, pl.ds(dff + f * tf, tf)],
                     cb_ref.at[0, :, pl.ds(dff + f * tf, tf)])
        return (0.5 * a * (1.0 + lax.erf(a * (1.0 / math.sqrt(2.0)))) * b).astype(BF16)

    ups, acts, acc = {}, {}, None
    for s in range(nf + 2):
        if s < nf:
            ups[s] = up(s)
        if 0 <= s - 1 < nf:
            acts[s - 1] = gate_act(s - 1, *ups.pop(s - 1))
        if 0 <= s - 2 < nf:
            y = jnp.dot(acts.pop(s - 2), wd_ref[0, pl.ds((s - 2) * tf, tf), :],
                        preferred_element_type=F32)
            acc = y if acc is None else acc + y
    gate = mod_ref[0, 0, 5:6, :]
    _store_lanes(o_ref, 0, _lanes(xm_ref) + gate * _rmsnorm(acc, gpost_ref[0]))


def _ffn(xs, mod, g_pre, g_post, w_up, dw_w, dw_b, w_down, layer, tf):
    nt, b, L, _ = xs.shape
    d = nt * LANES
    dff = w_down.shape[1]
    idx = lambda b_, i: (layer, 0, 0)
    once = dict(pipeline_mode=pl.Buffered(1))
    return pl.pallas_call(
        functools.partial(_ffn_kernel, tf=tf, nf=dff // tf),
        grid=(b, L // BLOCK_ROWS),
        in_specs=_block_halo_specs(nt, L) + [
            pl.BlockSpec((1, 1, N_MOD, d), lambda b_, i: (layer, b_, 0, 0)),
            pl.BlockSpec((1, 1, d), idx), pl.BlockSpec((1, 1, d), idx),
            pl.BlockSpec((1, d, 2 * dff), idx, **once),
            pl.BlockSpec((1, 3, 2 * dff), idx), pl.BlockSpec((1, 1, 2 * dff), idx),
            pl.BlockSpec((1, dff, d), idx, **once)],
        out_specs=pl.BlockSpec((nt, 1, BLOCK_ROWS, LANES), lambda b_, i: (0, b_, i, 0)),
        out_shape=jax.ShapeDtypeStruct(xs.shape, F32),
        scratch_shapes=[pltpu.VMEM((BLOCK_ROWS + 2 * HALO, d), BF16)],
        compiler_params=_params(("parallel", "parallel")),
        name="geglu_ffn",
    )(xs, xs, xs, mod, g_pre, g_post, w_up, dw_w, dw_b, w_down)


def kernel(x, c, ada_w, ada_b, g_mix_pre, g_mix_post, w_in, short_w, short_b, filt_w1, filt_b1,
           filt_w2, filt_b2, filt_freq, filt_w3, hyena_d, fnet_w, w_out, g_ffn_pre, g_ffn_post,
           w_up, dw_w, dw_b, w_down):
    b, L, d = x.shape
    depth = ada_w.shape[0]
    dh = hyena_d.shape[-1]
    df = w_in.shape[-1] - (HYENA_ORDER + 1) * dh
    cg = df // FNET_GROUPS
    assert b % 2 == 0 and L % BLOCK_ROWS == 0 and CONV_N2 % (L // CONV_N2) == 0
    assert d % LANES == 0 and dh % LANES == 0 and df % LANES == 0 and LANES % cg == 0
    tf = 2 * LANES

    row = lambda a: a.reshape(depth, 1, a.shape[-1])
    mod = _modulation(c, ada_w, ada_b).reshape(depth, b, N_MOD, d)

    w1p = jnp.pad(filt_w1, ((0, 0), (0, LANES - filt_w1.shape[1]), (0, 0)))
    kf = _filter_spectra(L, w1p, row(filt_b1), filt_w2, row(filt_b2), row(filt_freq),
                         filt_w3, dh)
    hd = hyena_d.reshape(depth, HYENA_ORDER, 1, dh)

    gpt = LANES // cg
    eye = jnp.eye(gpt, dtype=F32)
    wg = jnp.einsum("ltacd,ae->ltaced", fnet_w.reshape(depth, df // LANES, gpt, cg, cg), eye)
    wg = wg.reshape(depth, df // LANES, LANES, LANES).astype(BF16)

    w_in_b = w_in.astype(BF16)
    w_out_b = w_out.astype(BF16)
    w_up_b = w_up.astype(BF16)
    w_down_b = w_down.astype(BF16)
    g_mix_pre, g_mix_post = row(g_mix_pre), row(g_mix_post)
    g_ffn_pre, g_ffn_post = row(g_ffn_pre), row(g_ffn_post)
    short_b3, dw_b3 = row(short_b), row(dw_b)
    nct = dh // LANES

    xs = _to_block_slab(x)
    for l in range(depth):
        u, pf = _in_projection(xs, mod, g_mix_pre, w_in_b, short_w, short_b3, l)
        u5 = u.reshape(u.shape[0], 2, b // 2, L, LANES)
        z2 = _long_conv(u5, 0, u5, nct, kf, hd, l, 0)
        yh = _long_conv(z2, 0, u5, 2 * nct, kf, hd, l, 1).reshape(nct, b, L, LANES)
        yf = _fnet_mixer(pf, wg, l)
        xs = _out_projection(xs, yh, yf, mod, g_mix_post, w_out_b, l)
        xs = _ffn(xs, mod, g_ffn_pre, g_ffn_post, w_up_b, dw_w, dw_b3, w_down_b, l, tf)
    return _from_block_slab(xs)
```

```python
import functools
import math

import numpy as np
import jax
import jax.numpy as jnp
from jax import lax
from jax.experimental import pallas as pl
from jax.experimental.pallas import tpu as pltpu

F32 = jnp.float32
BF16 = jnp.bfloat16

LANES = 128
SUBLANES = 8
HALO = SUBLANES
VMEM_LIMIT = 56 * 1024 * 1024

FNET_GROUPS = 8
HYENA_ORDER = 2
N_DIRS = 2
FILTER_BANDS = 16
FILTER_HIDDEN = 64
DECAY_TARGET = 1e-2
FAST_DECAY_PCT = 0.3
SLOW_DECAY_PCT = 1.5
N_MOD = 6
NORM_EPS = 1e-6
FILTER_EPS = 1e-6

CONV_N2 = 128
BLOCK_ROWS = CONV_N2 * SUBLANES
SLAB_PAD = 8
SUB_ROWS = 256
UNROLL = 8


def _params(sem, vmem=VMEM_LIMIT):
    return pltpu.CompilerParams(dimension_semantics=sem, vmem_limit_bytes=vmem)


def _const_spec(shape):
    nd = len(shape)
    return pl.BlockSpec(shape, lambda *_: (0,) * nd, pipeline_mode=pl.Buffered(1))


@functools.lru_cache(maxsize=None)
def _conv_tables(L):
    n2n = CONV_N2
    nh = L // n2n
    n1n = 2 * nh
    N = 2 * L
    k1 = np.arange(n1n)[:, None]
    n1 = np.arange(nh)[None, :]
    tf = np.zeros((n2n, 2 * n1n, 2 * nh), np.float64)
    for n2 in range(n2n):
        th = 2.0 * np.pi * ((k1 * (n2n * n1 + n2)) % N) / N
        c, s = np.cos(th), np.sin(th)
        tf[n2] = np.block([[c, s], [-s, c]])
    ti = np.transpose(tf, (0, 2, 1))
    k2 = np.arange(n2n)[:, None]
    m = np.arange(n2n)[None, :]
    th2 = 2.0 * np.pi * ((k2 * m) % n2n) / n2n
    c2, s2 = np.cos(th2), np.sin(th2)
    m2 = np.block([[c2, s2], [-s2, c2]])
    return (jnp.asarray(tf.reshape(n2n * 2 * n1n, 2 * nh), BF16),
            jnp.asarray(ti.reshape(n2n * 2 * nh, 2 * n1n), BF16),
            jnp.asarray(m2, BF16), jnp.asarray(m2.T, BF16))


@functools.lru_cache(maxsize=None)
def _fnet_tables(L, cg):
    n2n = CONV_N2
    n1n = L // n2n
    q = n2n // n1n
    scale = 1.0 / math.sqrt(L * cg)
    j = np.arange(cg)
    thc = 2.0 * np.pi * ((j[:, None] * j[None, :]) % cg) / cg
    reps = LANES // cg
    cc = np.kron(np.eye(reps), np.cos(thc)) * scale
    sc = np.kron(np.eye(reps), np.sin(thc)) * scale
    cs = np.concatenate([cc, -sc], axis=1)
    k1 = np.arange(n1n)[:, None]
    n1 = np.arange(n1n)[None, :]
    tf = np.zeros((n2n, 2 * n1n, 2 * n1n), np.float64)
    for n2 in range(n2n):
        th = 2.0 * np.pi * ((k1 * (n2n * n1 + n2)) % L) / L
        c, s = np.cos(th), np.sin(th)
        tf[n2] = np.block([[c, s], [-s, c]])
    k2 = np.array([q * a + r for r in range(q) for a in range(n1n)])[:, None]
    m = np.arange(n2n)[None, :]
    th2 = 2.0 * np.pi * ((k2 * m) % n2n) / n2n
    m4 = np.concatenate([np.cos(th2), np.sin(th2)], axis=1)
    return (jnp.asarray(cs, BF16), jnp.asarray(tf.reshape(n2n * 2 * n1n, 2 * n1n), BF16),
            jnp.asarray(m4, BF16))


@functools.lru_cache(maxsize=None)
def _filter_features(L):
    pos = np.arange(L, dtype=np.float32)
    bands = np.linspace(1e-4, FILTER_BANDS - 1, FILTER_BANDS, dtype=np.float32)
    t = pos / np.float32(max(L - 1, 1))
    ang = np.float32(2.0 * math.pi / L) * pos[:, None] * bands[None, :]
    z = np.concatenate([t[:, None], np.cos(ang), -np.sin(ang)], axis=-1).astype(np.float32)
    zp = np.zeros((L, LANES), np.float32)
    zp[:, :z.shape[1]] = z
    return jnp.asarray(zp)


def _decay_rates(dh):
    min_decay = math.log(DECAY_TARGET) / SLOW_DECAY_PCT
    max_decay = math.log(DECAY_TARGET) / FAST_DECAY_PCT
    return jnp.abs(jnp.linspace(min_decay, max_decay, dh, dtype=F32)).reshape(1, dh)


def _loop(n, body):
    for i in range(n):
        body(i, 0)


def _aligned(i, m):
    return i if isinstance(i, int) else pl.multiple_of(i, m)


def _dot3(a, b):
    a_hi = a.astype(BF16)
    b_hi = b.astype(BF16)
    a_lo = (a - a_hi.astype(F32)).astype(BF16)
    b_lo = (b - b_hi.astype(F32)).astype(BF16)
    dot = functools.partial(jnp.dot, preferred_element_type=F32)
    return dot(a_hi, b_hi) + (dot(a_hi, b_lo) + dot(a_lo, b_hi))


def _modnorm(x, g, scale, shift):
    ms = jnp.mean(x * x, axis=-1, keepdims=True)
    return (x * lax.rsqrt(ms + NORM_EPS) * g) * (1.0 + scale) + shift


def _rmsnorm(y, g):
    ms = jnp.mean(y * y, axis=-1, keepdims=True)
    return y * lax.rsqrt(ms + NORM_EPS) * g


def _lanes(ref, rows=slice(None)):
    return jnp.concatenate([ref[j, 0, rows, :] for j in range(ref.shape[0])], axis=1)


def _store_lanes(ref, j0, v, rows=slice(None)):
    for j in range(v.shape[1] // LANES):
        ref[j0 + j, 0, rows, :] = v[:, j * LANES:(j + 1) * LANES]


def _halo_rows(xm_ref, xp_ref, xn_ref, g, scale, shift, first, last):
    hm = _modnorm(_lanes(xm_ref), g, scale, shift)
    hp = _modnorm(_lanes(xp_ref), g, scale, shift) * jnp.where(first, 0.0, 1.0)
    hn = _modnorm(_lanes(xn_ref), g, scale, shift) * jnp.where(last, 0.0, 1.0)
    return jnp.concatenate([hp, hm, hn], axis=0).astype(BF16)


def _dwconv3(p, w_ref, b_ref):
    rows = p.shape[0] - 2 * HALO
    cur = p[HALO:HALO + rows]
    sub = lax.broadcasted_iota(jnp.int32, (SUBLANES, p.shape[1]), 0)
    wrap_prev = jnp.where(sub == 0, pltpu.roll(p[:HALO], 1, 0),
                          pltpu.roll(cur[rows - SUBLANES:], 1, 0))
    wrap_next = jnp.where(sub == SUBLANES - 1, pltpu.roll(p[HALO + rows:], SUBLANES - 1, 0),
                          pltpu.roll(cur[:SUBLANES], SUBLANES - 1, 0))
    prev = jnp.concatenate([wrap_prev, cur[:rows - SUBLANES]], axis=0)
    nxt = jnp.concatenate([cur[SUBLANES:], wrap_next], axis=0)
    return b_ref[...] + prev * w_ref[0:1, :] + cur * w_ref[1:2, :] + nxt * w_ref[2:3, :]


def _slab(ref, idx, n2, nblk):
    return [ref[idx + (pl.ds(i * BLOCK_ROWS + n2 * SUBLANES, SUBLANES), slice(None))]
            for i in range(nblk)]


def _mod_kernel(c_ref, w_ref, b_ref, o_ref):
    c = c_ref[...]
    act = c * jax.nn.sigmoid(c)
    o_ref[0] = jnp.dot(act, w_ref[0], preferred_element_type=F32,
                       precision=lax.Precision.HIGHEST) + b_ref[0]


def _modulation(c, ada_w, ada_b):
    depth, d, nm = ada_w.shape
    b = c.shape[0]
    tn = nm // 4
    return pl.pallas_call(
        _mod_kernel,
        grid=(depth, nm // tn),
        in_specs=[pl.BlockSpec((b, d), lambda l, j: (0, 0)),
                  pl.BlockSpec((1, d, tn), lambda l, j: (l, 0, j)),
                  pl.BlockSpec((1, 1, tn), lambda l, j: (l, 0, j))],
        out_specs=pl.BlockSpec((1, b, tn), lambda l, j: (l, 0, j)),
        out_shape=jax.ShapeDtypeStruct((depth, b, nm), F32),
        compiler_params=_params(("parallel", "parallel")),
        name="adaln_modulation",
    )(c, ada_w, ada_b.reshape(depth, 1, nm))


def _stage1(load_x, t_ref, s1_refs, n_iter, rows_out, pitch, k_in=None):
    def body(n2, carry):
        t = t_ref[pl.ds(_aligned(n2 * rows_out, rows_out), rows_out), :]
        if k_in is not None:
            t = t[:, :k_in]
        a = jnp.dot(t, load_x(n2), preferred_element_type=F32)
        for c, s1_ref in enumerate(s1_refs):
            s1_ref[pl.ds(_aligned(n2 * pitch, SUBLANES), rows_out), :] = (
                a[:, c * LANES:(c + 1) * LANES])
        return carry
    _loop(n_iter, body)


def _load_slab_column(s1_ref, k1, n1n, n2n, pitch):
    ar = s1_ref[pl.ds(k1, n2n, stride=pitch), :]
    ai = s1_ref[pl.ds(n1n + k1, n2n, stride=pitch), :]
    return jnp.concatenate([ar, ai], axis=0).astype(BF16)


def _filter_kernel(zf_ref, w1_ref, b1_ref, w2_ref, b2_ref, fr_ref, w3f_ref, w3b_ref, dl_ref,
                   tf_ref, m2_ref, kf_ref, h_ref, fwd_ref, bwd_ref, s1_ref, s1b_ref, *, L):
    n2n = CONV_N2
    nh = L // n2n
    n1n = 2 * nh
    pitch = 2 * n1n + SLAB_PAD

    @pl.when(pl.program_id(1) == 0)
    def _():
        fr = fr_ref[0]
        h = jnp.sin(fr * (_dot3(zf_ref[...], w1_ref[0]) + b1_ref[0]))
        h_ref[...] = jnp.sin(fr * (_dot3(h, w2_ref[0]) + b2_ref[0]))

    h = h_ref[...]
    hf = _dot3(h, w3f_ref[0])
    hb = _dot3(h, w3b_ref[0])
    pos = lax.broadcasted_iota(jnp.int32, (L, LANES), 0)
    t = pos.astype(F32) / float(max(L - 1, 1))
    decay = jnp.exp(-t * dl_ref[...])
    hf = hf * decay
    hb = jnp.where(pos == 0, 0.0, hb * decay)
    norm = (jnp.sum(jnp.abs(hf), axis=0, keepdims=True)
            + jnp.sum(jnp.abs(hb), axis=0, keepdims=True) + FILTER_EPS)
    inv = 1.0 / (norm * float(2 * L))
    fwd_ref[...] = hf * inv
    bwd_ref[...] = hb * inv

    def load_x(n2):
        return jnp.concatenate([fwd_ref[pl.ds(n2, nh, stride=n2n), :],
                                bwd_ref[pl.ds(n2, nh, stride=n2n), :]], axis=1).astype(BF16)
    _stage1(load_x, tf_ref, [s1_ref, s1b_ref], n2n, 2 * n1n, pitch, k_in=nh)

    def body(k1, carry):
        a = jnp.concatenate([_load_slab_column(s1_ref, k1, n1n, n2n, pitch),
                             _load_slab_column(s1b_ref, k1, n1n, n2n, pitch)], axis=1)
        u = jnp.dot(m2_ref[...], a, preferred_element_type=F32)
        base = _aligned(k1 * 2 * n2n, 2 * n2n)
        kf_ref[0, 0, pl.ds(base, n2n), :] = u[:n2n, :LANES] + u[:n2n, LANES:]
        kf_ref[0, 0, pl.ds(base + n2n, n2n), :] = u[n2n:, :LANES] - u[n2n:, LANES:]
        return carry
    _loop(n1n, body)


def _filter_spectra(L, w1p, b1, w2, b2, freq, w3, dh):
    depth = w1p.shape[0]
    n2n = CONV_N2
    nh = L // n2n
    n1n = 2 * nh
    pitch = 2 * n1n + SLAB_PAD
    tf, _, m2, _ = _conv_tables(L)
    zf = _filter_features(L)
    ct = dh // LANES
    fh = FILTER_HIDDEN
    cols = N_DIRS * ct

    lyr = lambda l, i: (l, 0, 0)
    return pl.pallas_call(
        functools.partial(_filter_kernel, L=L),
        grid=(depth, HYENA_ORDER * ct),
        in_specs=[_const_spec(zf.shape),
                  pl.BlockSpec((1, LANES, fh), lyr), pl.BlockSpec((1, 1, fh), lyr),
                  pl.BlockSpec((1, fh, fh), lyr), pl.BlockSpec((1, 1, fh), lyr),
                  pl.BlockSpec((1, 1, fh), lyr),
                  pl.BlockSpec((1, fh, LANES), lambda l, i: (l, 0, (i // ct) * cols + i % ct)),
                  pl.BlockSpec((1, fh, LANES),
                               lambda l, i: (l, 0, (i // ct) * cols + ct + i % ct)),
                  pl.BlockSpec((1, LANES), lambda l, i: (0, i % ct)),
                  _const_spec(tf.shape), _const_spec(m2.shape)],
        out_specs=pl.BlockSpec((1, 1, n1n * 2 * n2n, LANES),
                               lambda l, i: (l, i // ct, 0, i % ct)),
        out_shape=jax.ShapeDtypeStruct((depth, HYENA_ORDER, n1n * 2 * n2n, dh), F32),
        scratch_shapes=[pltpu.VMEM((L, fh), F32),
                        pltpu.VMEM((L, LANES), F32), pltpu.VMEM((L, LANES), F32),
                        pltpu.VMEM((n2n * pitch, LANES), F32),
                        pltpu.VMEM((n2n * pitch, LANES), F32)],
        compiler_params=_params(("parallel", "arbitrary")),
        name="hyena_filter_spectrum",
    )(zf, w1p, b1, w2, b2, freq, w3, w3, _decay_rates(dh), tf, m2)


def _conv_kernel(z_ref, g_ref, kf_ref, d_ref, tf_ref, ti_ref, m2_ref, m2i_ref, o_ref, s1_ref,
                 *, L):
    n2n = CONV_N2
    nh = L // n2n
    n1n = 2 * nh
    pitch = 2 * n1n + SLAB_PAD
    nblk = L // BLOCK_ROWS

    def load_x(n2):
        return jnp.concatenate(_slab(z_ref, (0, 0, 0), n2, nblk)
                               + _slab(z_ref, (0, 1, 0), n2, nblk), axis=0).astype(BF16)
    _stage1(load_x, tf_ref, [s1_ref], n2n, 2 * n1n, pitch)

    def mid(j, carry):
        cols = [j * UNROLL + i for i in range(UNROLL)]
        a = [_load_slab_column(s1_ref, k1, n1n, n2n, pitch) for k1 in cols]
        w = []
        for i in range(0, UNROLL, 2):
            u = jnp.dot(m2_ref[...], jnp.concatenate(a[i:i + 2], axis=1),
                        preferred_element_type=F32)
            v = []
            for h in range(2):
                ur, ui = u[:n2n, h * LANES:(h + 1) * LANES], u[n2n:, h * LANES:(h + 1) * LANES]
                base = _aligned(cols[i + h] * 2 * n2n, 2 * n2n)
                kr = kf_ref[0, 0, pl.ds(base, n2n), :]
                ki = kf_ref[0, 0, pl.ds(base + n2n, n2n), :]
                v.append(jnp.concatenate([ur * kr - ui * ki, ur * ki + ui * kr], axis=0))
            wk = jnp.dot(m2i_ref[...], jnp.concatenate(v, axis=1).astype(BF16),
                         preferred_element_type=F32)
            w += [wk[:, :LANES], wk[:, LANES:]]
        for k1, wk in zip(cols, w):
            s1_ref[pl.ds(k1, n2n, stride=pitch), :] = wk[:n2n]
            s1_ref[pl.ds(n1n + k1, n2n, stride=pitch), :] = wk[n2n:]
        return carry
    _loop(n1n // UNROLL, mid)

    d = d_ref[0, 0]

    def last(n2, carry):
        w = s1_ref[pl.ds(_aligned(n2 * pitch, SUBLANES), 2 * n1n), :].astype(BF16)
        t = ti_ref[pl.ds(_aligned(n2 * 2 * nh, 2 * nh), 2 * nh), :]
        y = jnp.dot(t, w, preferred_element_type=F32)
        for half in range(2):
            zz = _slab(z_ref, (0, half, 0), n2, nblk)
            gg = _slab(g_ref, (0, half, 0), n2, nblk)
            for i in range(nblk):
                r0 = half * nh + i * SUBLANES
                o_ref[0, half, 0, pl.ds(i * BLOCK_ROWS + n2 * SUBLANES, SUBLANES), :] = (
                    gg[i] * (y[r0:r0 + SUBLANES] + d * zz[i]))
        return carry
    _loop(n2n, last)


def _long_conv(z, z_col0, gate, gate_col0, kf, d, layer, order):
    _, _, bh, L, _ = z.shape
    dh = kf.shape[-1]
    ct = dh // LANES
    n2n = CONV_N2
    nh = L // n2n
    n1n = 2 * nh
    pitch = 2 * n1n + SLAB_PAD
    tf, ti, m2, m2i = _conv_tables(L)
    blk = (1, 2, 1, L, LANES)
    return pl.pallas_call(
        functools.partial(_conv_kernel, L=L),
        grid=(ct, bh),
        in_specs=[pl.BlockSpec(blk, lambda c, p: (z_col0 + c, 0, p, 0, 0)),
                  pl.BlockSpec(blk, lambda c, p: (gate_col0 + c, 0, p, 0, 0)),
                  pl.BlockSpec((1, 1, n1n * 2 * n2n, LANES), lambda c, p: (layer, order, 0, c),
                               pipeline_mode=pl.Buffered(1)),
                  pl.BlockSpec((1, 1, 1, LANES), lambda c, p: (layer, order, 0, c)),
                  _const_spec(tf.shape), _const_spec(ti.shape),
                  _const_spec(m2.shape), _const_spec(m2i.shape)],
        out_specs=pl.BlockSpec(blk, lambda c, p: (c, 0, p, 0, 0)),
        out_shape=jax.ShapeDtypeStruct((ct, 2, bh, L, LANES), F32),
        scratch_shapes=[pltpu.VMEM((n2n * pitch, LANES), F32)],
        compiler_params=_params(("parallel", "parallel")),
        name="hyena_long_conv",
    )(z, gate, kf, d, tf, ti, m2, m2i)


def _fnet_kernel(p_ref, cs_ref, tf_ref, m4_ref, wg_ref, o_ref, a_ref, b_ref, s1_ref, *, L):
    n2n = CONV_N2
    n1n = L // n2n
    pitch = 2 * n1n + SLAB_PAD
    nblk = L // BLOCK_ROWS
    nct = p_ref.shape[0]
    for c in range(nct):
        wg = wg_ref[0, c]
        csw = jnp.concatenate(
            [jnp.dot(cs_ref[:, h * LANES:(h + 1) * LANES], wg, preferred_element_type=F32)
             for h in range(2)], axis=1).astype(BF16)
        ab = jnp.dot(p_ref[c, 0].astype(BF16), csw, preferred_element_type=F32)
        a_ref[c] = ab[:, :LANES]
        b_ref[c] = ab[:, LANES:]

    def load_x(n2):
        return jnp.concatenate(
            [jnp.concatenate(_slab(a_ref, (c,), n2, nblk) + _slab(b_ref, (c,), n2, nblk), axis=0)
             for c in range(nct)], axis=1).astype(BF16)
    _stage1(load_x, tf_ref, [s1_ref.at[c] for c in range(nct)], n2n, 2 * n1n, pitch)

    def body(k1, carry):
        a = jnp.concatenate([_load_slab_column(s1_ref.at[c], k1, n1n, n2n, pitch)
                             for c in range(nct)], axis=1)
        y = jnp.dot(m4_ref[...], a, preferred_element_type=F32)
        for c in range(nct):
            for rr in range(n2n // n1n):
                for i in range(nblk):
                    src = rr * n1n + i * SUBLANES
                    dst = i * BLOCK_ROWS + (k1 + n1n * rr) * SUBLANES
                    o_ref[c, 0, pl.ds(dst, SUBLANES), :] = y[src:src + SUBLANES,
                                                             c * LANES:(c + 1) * LANES]
        return carry
    _loop(n1n, body)


def _fnet_mixer(pf, wg, layer):
    ct, b, L, _ = pf.shape
    cg = ct * LANES // FNET_GROUPS
    cs, tf, m4 = _fnet_tables(L, cg)
    n2n = CONV_N2
    pitch = 2 * (L // n2n) + SLAB_PAD
    tiles = 2 if ct % 2 == 0 else 1
    return pl.pallas_call(
        functools.partial(_fnet_kernel, L=L),
        grid=(b, ct // tiles),
        in_specs=[pl.BlockSpec((tiles, 1, L, LANES), lambda i, c: (c, i, 0, 0)),
                  _const_spec(cs.shape), _const_spec(tf.shape), _const_spec(m4.shape),
                  pl.BlockSpec((1, tiles, LANES, LANES), lambda i, c: (layer, c, 0, 0))],
        out_specs=pl.BlockSpec((tiles, 1, L, LANES), lambda i, c: (c, i, 0, 0)),
        out_shape=jax.ShapeDtypeStruct((ct, b, L, LANES), F32),
        scratch_shapes=[pltpu.VMEM((tiles, L, LANES), F32), pltpu.VMEM((tiles, L, LANES), F32),
                        pltpu.VMEM((tiles, n2n * pitch, LANES), F32)],
        compiler_params=_params(("parallel", "parallel")),
        name="fnet_mixer",
    )(pf, cs, tf, m4, wg)


def _to_block_slab_kernel(x_ref, o_ref):
    for j in range(o_ref.shape[0]):
        flat = o_ref.at[j, 0]
        for r in range(SUBLANES):
            flat[pl.ds(r, CONV_N2, stride=SUBLANES), :] = (
                x_ref[0, r * CONV_N2:(r + 1) * CONV_N2, j * LANES:(j + 1) * LANES])


def _from_block_slab_kernel(x_ref, o_ref):
    for j in range(x_ref.shape[0]):
        flat = x_ref.at[j, 0]
        for r in range(SUBLANES):
            o_ref[0, r * CONV_N2:(r + 1) * CONV_N2, j * LANES:(j + 1) * LANES] = (
                flat[pl.ds(r, CONV_N2, stride=SUBLANES), :])


def _to_block_slab(x):
    b, L, d = x.shape
    return pl.pallas_call(
        _to_block_slab_kernel,
        grid=(b, L // BLOCK_ROWS),
        in_specs=[pl.BlockSpec((1, BLOCK_ROWS, d), lambda b_, i: (b_, i, 0))],
        out_specs=pl.BlockSpec((d // LANES, 1, BLOCK_ROWS, LANES), lambda b_, i: (0, b_, i, 0)),
        out_shape=jax.ShapeDtypeStruct((d // LANES, b, L, LANES), F32),
        compiler_params=_params(("parallel", "parallel")),
        name="to_block_slab_order",
    )(x)


def _from_block_slab(xs):
    nt, b, L, _ = xs.shape
    return pl.pallas_call(
        _from_block_slab_kernel,
        grid=(b, L // BLOCK_ROWS),
        in_specs=[pl.BlockSpec((nt, 1, BLOCK_ROWS, LANES), lambda b_, i: (0, b_, i, 0))],
        out_specs=pl.BlockSpec((1, BLOCK_ROWS, nt * LANES), lambda b_, i: (b_, i, 0)),
        out_shape=jax.ShapeDtypeStruct((b, L, nt * LANES), F32),
        compiler_params=_params(("parallel", "parallel")),
        name="from_block_slab_order",
    )(xs)


def _block_halo_specs(nt, L):
    nslab = L // SUBLANES
    step = BLOCK_ROWS // SUBLANES
    return [pl.BlockSpec((nt, 1, BLOCK_ROWS, LANES), lambda b, i: (0, b, i, 0)),
            pl.BlockSpec((nt, 1, SUBLANES, LANES),
                         lambda b, i: (0, b, jnp.maximum(i * step - 1, 0), 0)),
            pl.BlockSpec((nt, 1, SUBLANES, LANES),
                         lambda b, i: (0, b, jnp.minimum((i + 1) * step, nslab - 1), 0))]


def _inproj_kernel(xm_ref, xp_ref, xn_ref, mod_ref, g_ref, w_ref, sw_ref, sb_ref,
                   u_ref, pf_ref, h_ref, *, tcol):
    i = pl.program_id(1)
    shift, scale = mod_ref[0, 0, 0:1, :], mod_ref[0, 0, 1:2, :]
    h_ref[...] = _halo_rows(xm_ref, xp_ref, xn_ref, g_ref[0], scale, shift,
                            i == 0, i == pl.num_programs(1) - 1)
    nconv = u_ref.shape[0] * LANES // tcol
    ncol = nconv + pf_ref.shape[0] * LANES // tcol
    ps = {}
    for s in range(ncol + 1):
        if s < ncol:
            ps[s] = jnp.dot(h_ref[...], w_ref[0, :, pl.ds(s * tcol, tcol)],
                            preferred_element_type=F32)
        if s >= 1:
            f = s - 1
            p = ps.pop(f)
            if f < nconv:
                cols = pl.ds(f * tcol, tcol)
                _store_lanes(u_ref, f * tcol // LANES,
                             _dwconv3(p, sw_ref.at[0, :, cols], sb_ref.at[0, :, cols]))
            else:
                _store_lanes(pf_ref, (f - nconv) * tcol // LANES, p[HALO:HALO + BLOCK_ROWS])


def _in_projection(xs, mod, g_pre, w_in, short_w, short_b, layer):
    nt, b, L, _ = xs.shape
    d = nt * LANES
    e = w_in.shape[-1]
    dconv = short_w.shape[-1]
    out_blk = lambda n: pl.BlockSpec((n, 1, BLOCK_ROWS, LANES), lambda b_, i: (0, b_, i, 0))
    return pl.pallas_call(
        functools.partial(_inproj_kernel, tcol=2 * LANES),
        grid=(b, L // BLOCK_ROWS),
        in_specs=_block_halo_specs(nt, L) + [
            pl.BlockSpec((1, 1, N_MOD, d), lambda b_, i: (layer, b_, 0, 0)),
            pl.BlockSpec((1, 1, d), lambda b_, i: (layer, 0, 0)),
            pl.BlockSpec((1, d, e), lambda b_, i: (layer, 0, 0), pipeline_mode=pl.Buffered(1)),
            pl.BlockSpec((1, 3, dconv), lambda b_, i: (layer, 0, 0)),
            pl.BlockSpec((1, 1, dconv), lambda b_, i: (layer, 0, 0))],
        out_specs=[out_blk(dconv // LANES), out_blk((e - dconv) // LANES)],
        out_shape=[jax.ShapeDtypeStruct((dconv // LANES, b, L, LANES), F32),
                   jax.ShapeDtypeStruct(((e - dconv) // LANES, b, L, LANES), F32)],
        scratch_shapes=[pltpu.VMEM((BLOCK_ROWS + 2 * HALO, d), BF16)],
        compiler_params=_params(("parallel", "parallel")),
        name="in_projection",
    )(xs, xs, xs, mod, g_pre, w_in, short_w, short_b)


def _outproj_kernel(x_ref, yh_ref, yf_ref, mod_ref, g_ref, w_ref, o_ref):
    gate = mod_ref[0, 0, 2:3, :]
    g = g_ref[0]
    ycs, ys = {}, {}
    nsub = BLOCK_ROWS // SUB_ROWS
    for s in range(nsub + 2):
        if s < nsub:
            rows = pl.ds(s * SUB_ROWS, SUB_ROWS)
            ycs[s] = jnp.concatenate([_lanes(yh_ref, rows), _lanes(yf_ref, rows)],
                                     axis=1).astype(BF16)
        if 0 <= s - 1 < nsub:
            ys[s - 1] = jnp.dot(ycs.pop(s - 1), w_ref[0], preferred_element_type=F32)
        if 0 <= s - 2 < nsub:
            rows = pl.ds((s - 2) * SUB_ROWS, SUB_ROWS)
            _store_lanes(o_ref, 0, _lanes(x_ref, rows) + gate * _rmsnorm(ys.pop(s - 2), g), rows)


def _out_projection(xs, yh, yf, mod, g_post, w_out, layer):
    nt, b, L, _ = xs.shape
    d = nt * LANES
    blk = lambda n: pl.BlockSpec((n, 1, BLOCK_ROWS, LANES), lambda b_, i: (0, b_, i, 0))
    return pl.pallas_call(
        _outproj_kernel,
        grid=(b, L // BLOCK_ROWS),
        in_specs=[blk(nt), blk(yh.shape[0]), blk(yf.shape[0]),
                  pl.BlockSpec((1, 1, N_MOD, d), lambda b_, i: (layer, b_, 0, 0)),
                  pl.BlockSpec((1, 1, d), lambda b_, i: (layer, 0, 0)),
                  pl.BlockSpec((1, (yh.shape[0] + yf.shape[0]) * LANES, d),
                               lambda b_, i: (layer, 0, 0), pipeline_mode=pl.Buffered(1))],
        out_specs=blk(nt),
        out_shape=jax.ShapeDtypeStruct(xs.shape, F32),
        compiler_params=_params(("parallel", "parallel")),
        name="out_projection",
    )(xs, yh, yf, mod, g_post, w_out)


def _ffn_kernel(xm_ref, xp_ref, xn_ref, mod_ref, gpre_ref, gpost_ref, wu_ref, cw_ref, cb_ref,
                wd_ref, o_ref, h_ref, *, tf, nf):
    i = pl.program_id(1)
    shift, scale = mod_ref[0, 0, 3:4, :], mod_ref[0, 0, 4:5, :]
    h_ref[...] = _halo_rows(xm_ref, xp_ref, xn_ref, gpre_ref[0], scale, shift,
                            i == 0, i == pl.num_programs(1) - 1)
    dff = nf * tf

    def up(f):
        h = h_ref[...]
        return [jnp.dot(h, wu_ref[0, :, pl.ds(c0, tf)], preferred_element_type=F32)
                for c0 in (f * tf, dff + f * tf)]

    def gate_act(f, pa, pb):
        a = _dwconv3(pa, cw_ref.at[0, :, pl.ds(f * tf, tf)], cb_ref.at[0, :, pl.ds(f * tf, tf)])
        b = _dwconv3(pb, cw_ref.at[0, :, pl.ds(dff + f * tf, tf)],
                     cb_ref.at[0, :, pl.ds(dff + f * tf, tf)])
        return (0.5 * a * (1.0 + lax.erf(a * (1.0 / math.sqrt(2.0)))) * b).astype(BF16)

    ups, acts, acc = {}, {}, None
    for s in range(nf + 2):
        if s < nf:
            ups[s] = up(s)
        if 0 <= s - 1 < nf:
            acts[s - 1] = gate_act(s - 1, *ups.pop(s - 1))
        f = s - 2
        if 0 <= f < nf and (f % 2 == 1 or f == nf - 1):
            f0 = f - f % 2
            act = jnp.concatenate([acts.pop(k) for k in range(f0, f + 1)], axis=1)
            y = jnp.dot(act, wd_ref[0, pl.ds(f0 * tf, act.shape[1]), :],
                        preferred_element_type=F32)
            acc = y if acc is None else acc + y
    gate = mod_ref[0, 0, 5:6, :]
    _store_lanes(o_ref, 0, _lanes(xm_ref) + gate * _rmsnorm(acc, gpost_ref[0]))


def _ffn(xs, mod, g_pre, g_post, w_up, dw_w, dw_b, w_down, layer, tf):
    nt, b, L, _ = xs.shape
    d = nt * LANES
    dff = w_down.shape[1]
    idx = lambda b_, i: (layer, 0, 0)
    once = dict(pipeline_mode=pl.Buffered(1))
    return pl.pallas_call(
        functools.partial(_ffn_kernel, tf=tf, nf=dff // tf),
        grid=(b, L // BLOCK_ROWS),
        in_specs=_block_halo_specs(nt, L) + [
            pl.BlockSpec((1, 1, N_MOD, d), lambda b_, i: (layer, b_, 0, 0)),
            pl.BlockSpec((1, 1, d), idx), pl.BlockSpec((1, 1, d), idx),
            pl.BlockSpec((1, d, 2 * dff), idx, **once),
            pl.BlockSpec((1, 3, 2 * dff), idx), pl.BlockSpec((1, 1, 2 * dff), idx),
            pl.BlockSpec((1, dff, d), idx, **once)],
        out_specs=pl.BlockSpec((nt, 1, BLOCK_ROWS, LANES), lambda b_, i: (0, b_, i, 0)),
        out_shape=jax.ShapeDtypeStruct(xs.shape, F32),
        scratch_shapes=[pltpu.VMEM((BLOCK_ROWS + 2 * HALO, d), BF16)],
        compiler_params=_params(("parallel", "parallel")),
        name="geglu_ffn",
    )(xs, xs, xs, mod, g_pre, g_post, w_up, dw_w, dw_b, w_down)


def kernel(x, c, ada_w, ada_b, g_mix_pre, g_mix_post, w_in, short_w, short_b, filt_w1, filt_b1,
           filt_w2, filt_b2, filt_freq, filt_w3, hyena_d, fnet_w, w_out, g_ffn_pre, g_ffn_post,
           w_up, dw_w, dw_b, w_down):
    b, L, d = x.shape
    depth = ada_w.shape[0]
    dh = hyena_d.shape[-1]
    df = w_in.shape[-1] - (HYENA_ORDER + 1) * dh
    cg = df // FNET_GROUPS
    assert b % 2 == 0 and L % BLOCK_ROWS == 0 and CONV_N2 % (L // CONV_N2) == 0
    assert d % LANES == 0 and dh % LANES == 0 and df % LANES == 0 and LANES % cg == 0
    tf = 2 * LANES

    row = lambda a: a.reshape(depth, 1, a.shape[-1])
    mod = _modulation(c, ada_w, ada_b).reshape(depth, b, N_MOD, d)

    w1p = jnp.pad(filt_w1, ((0, 0), (0, LANES - filt_w1.shape[1]), (0, 0)))
    kf = _filter_spectra(L, w1p, row(filt_b1), filt_w2, row(filt_b2), row(filt_freq),
                         filt_w3, dh)
    hd = hyena_d.reshape(depth, HYENA_ORDER, 1, dh)

    gpt = LANES // cg
    eye = jnp.eye(gpt, dtype=F32)
    wg = jnp.einsum("ltacd,ae->ltaced", fnet_w.reshape(depth, df // LANES, gpt, cg, cg), eye)
    wg = wg.reshape(depth, df // LANES, LANES, LANES).astype(BF16)

    w_in_b = w_in.astype(BF16)
    w_out_b = w_out.astype(BF16)
    w_up_b = w_up.astype(BF16)
    w_down_b = w_down.astype(BF16)
    g_mix_pre, g_mix_post = row(g_mix_pre), row(g_mix_post)
    g_ffn_pre, g_ffn_post = row(g_ffn_pre), row(g_ffn_post)
    short_b3, dw_b3 = row(short_b), row(dw_b)
    nct = dh // LANES

    xs = _to_block_slab(x)
    for l in range(depth):
        u, pf = _in_projection(xs, mod, g_mix_pre, w_in_b, short_w, short_b3, l)
        u5 = u.reshape(u.shape[0], 2, b // 2, L, LANES)
        z2 = _long_conv(u5, 0, u5, nct, kf, hd, l, 0)
        yh = _long_conv(z2, 0, u5, 2 * nct, kf, hd, l, 1).reshape(nct, b, L, LANES)
        yf = _fnet_mixer(pf, wg, l)
        xs = _out_projection(xs, yh, yf, mod, g_mix_post, w_out_b, l)
        xs = _ffn(xs, mod, g_ffn_pre, g_ffn_post, w_up_b, dw_w, dw_b3, w_down_b, l, tf)
    return _from_block_slab(xs)
```

```python
import functools
import math

import numpy as np
import jax
import jax.numpy as jnp
from jax import lax
from jax.experimental import pallas as pl
from jax.experimental.pallas import tpu as pltpu

F32 = jnp.float32
BF16 = jnp.bfloat16

LANES = 128
SUBLANES = 8
HALO = SUBLANES
VMEM_LIMIT = 56 * 1024 * 1024

FNET_GROUPS = 8
HYENA_ORDER = 2
N_DIRS = 2
FILTER_BANDS = 16
FILTER_HIDDEN = 64
DECAY_TARGET = 1e-2
FAST_DECAY_PCT = 0.3
SLOW_DECAY_PCT = 1.5
N_MOD = 6
NORM_EPS = 1e-6
FILTER_EPS = 1e-6

CONV_N2 = 128
BLOCK_ROWS = CONV_N2 * SUBLANES
SLAB_PAD = 8
SUB_ROWS = 256
NCHUNK = 4
UNROLL = 8


def _params(sem, vmem=VMEM_LIMIT):
    return pltpu.CompilerParams(dimension_semantics=sem, vmem_limit_bytes=vmem)


def _const_spec(shape):
    nd = len(shape)
    return pl.BlockSpec(shape, lambda *_: (0,) * nd, pipeline_mode=pl.Buffered(1))


@functools.lru_cache(maxsize=None)
def _conv_tables(L):
    n2n = CONV_N2
    nh = L // n2n
    n1n = 2 * nh
    N = 2 * L
    k1 = np.arange(n1n)[:, None]
    n1 = np.arange(nh)[None, :]
    tf = np.zeros((n2n, 2 * n1n, 2 * nh), np.float64)
    for n2 in range(n2n):
        th = 2.0 * np.pi * ((k1 * (n2n * n1 + n2)) % N) / N
        c, s = np.cos(th), np.sin(th)
        tf[n2] = np.block([[c, s], [-s, c]])
    ti = np.transpose(tf, (0, 2, 1))
    k2 = np.arange(n2n)[:, None]
    m = np.arange(n2n)[None, :]
    th2 = 2.0 * np.pi * ((k2 * m) % n2n) / n2n
    c2, s2 = np.cos(th2), np.sin(th2)
    m2 = np.block([[c2, s2], [-s2, c2]])
    return (jnp.asarray(tf.reshape(n2n * 2 * n1n, 2 * nh), BF16),
            jnp.asarray(ti.reshape(n2n * 2 * nh, 2 * n1n), BF16),
            jnp.asarray(m2, BF16), jnp.asarray(m2.T, BF16))


@functools.lru_cache(maxsize=None)
def _fnet_tables(L, cg):
    n2n = CONV_N2
    n1n = L // n2n
    q = n2n // n1n
    scale = 1.0 / math.sqrt(L * cg)
    j = np.arange(cg)
    thc = 2.0 * np.pi * ((j[:, None] * j[None, :]) % cg) / cg
    reps = LANES // cg
    cc = np.kron(np.eye(reps), np.cos(thc)) * scale
    sc = np.kron(np.eye(reps), np.sin(thc)) * scale
    cs = np.concatenate([cc, -sc], axis=1)
    k1 = np.arange(n1n)[:, None]
    n1 = np.arange(n1n)[None, :]
    tf = np.zeros((n2n, 2 * n1n, 2 * n1n), np.float64)
    for n2 in range(n2n):
        th = 2.0 * np.pi * ((k1 * (n2n * n1 + n2)) % L) / L
        c, s = np.cos(th), np.sin(th)
        tf[n2] = np.block([[c, s], [-s, c]])
    k2 = np.array([q * a + r for r in range(q) for a in range(n1n)])[:, None]
    m = np.arange(n2n)[None, :]
    th2 = 2.0 * np.pi * ((k2 * m) % n2n) / n2n
    m4 = np.concatenate([np.cos(th2), np.sin(th2)], axis=1)
    return (jnp.asarray(cs, BF16), jnp.asarray(tf.reshape(n2n * 2 * n1n, 2 * n1n), BF16),
            jnp.asarray(m4, BF16))


@functools.lru_cache(maxsize=None)
def _filter_features(L):
    pos = np.arange(L, dtype=np.float32)
    bands = np.linspace(1e-4, FILTER_BANDS - 1, FILTER_BANDS, dtype=np.float32)
    t = pos / np.float32(max(L - 1, 1))
    ang = np.float32(2.0 * math.pi / L) * pos[:, None] * bands[None, :]
    z = np.concatenate([t[:, None], np.cos(ang), -np.sin(ang)], axis=-1).astype(np.float32)
    zp = np.zeros((L, LANES), np.float32)
    zp[:, :z.shape[1]] = z
    zp = zp.reshape(L // CONV_N2, CONV_N2, LANES).transpose(1, 0, 2).reshape(L, LANES)
    return jnp.asarray(zp)


def _decay_rates(dh):
    min_decay = math.log(DECAY_TARGET) / SLOW_DECAY_PCT
    max_decay = math.log(DECAY_TARGET) / FAST_DECAY_PCT
    return jnp.abs(jnp.linspace(min_decay, max_decay, dh, dtype=F32)).reshape(1, dh)


def _loop(n, body):
    for i in range(n):
        body(i, 0)


def _aligned(i, m):
    return i if isinstance(i, int) else pl.multiple_of(i, m)


def _dot3(a, b):
    a_hi = a.astype(BF16)
    b_hi = b.astype(BF16)
    a_lo = (a - a_hi.astype(F32)).astype(BF16)
    b_lo = (b - b_hi.astype(F32)).astype(BF16)
    dot = functools.partial(jnp.dot, preferred_element_type=F32)
    return dot(a_hi, b_hi) + (dot(a_hi, b_lo) + dot(a_lo, b_hi))


def _modnorm(x, g, scale, shift):
    ms = jnp.mean(x * x, axis=-1, keepdims=True)
    return (x * lax.rsqrt(ms + NORM_EPS) * g) * (1.0 + scale) + shift


def _rmsnorm(y, g):
    ms = jnp.mean(y * y, axis=-1, keepdims=True)
    return y * lax.rsqrt(ms + NORM_EPS) * g


def _lanes(ref, rows=slice(None)):
    return jnp.concatenate([ref[j, 0, rows, :] for j in range(ref.shape[0])], axis=1)


def _store_lanes(ref, j0, v, rows=slice(None)):
    for j in range(v.shape[1] // LANES):
        ref[j0 + j, 0, rows, :] = v[:, j * LANES:(j + 1) * LANES]


def _halo_rows(xm_ref, xp_ref, xn_ref, h_ref, g, scale, shift, first, last):
    rows = xm_ref.shape[2]
    step = rows // NCHUNK
    pad = 2 * HALO
    edges = [0] + [pad - HALO + step * c for c in range(1, NCHUNK)] + [rows]
    chunks, r0 = [], 0
    for c in range(NCHUNK):
        part = [_modnorm(_lanes(xm_ref, pl.ds(edges[c], edges[c + 1] - edges[c])),
                         g, scale, shift)]
        if c == 0:
            part.insert(0, _modnorm(_lanes(xp_ref), g, scale, shift) * jnp.where(first, 0.0, 1.0))
        if c == NCHUNK - 1:
            part.append(_modnorm(_lanes(xn_ref), g, scale, shift) * jnp.where(last, 0.0, 1.0))
        chunk = jnp.concatenate(part, axis=0).astype(BF16) if len(part) > 1 else part[0].astype(BF16)
        h_ref[pl.ds(r0, chunk.shape[0]), :] = chunk
        chunks.append(chunk)
        r0 += chunk.shape[0]
    return chunks


def _dwconv3(p, w_ref, b_ref):
    rows = p.shape[0] - 2 * HALO
    cur = p[HALO:HALO + rows]
    sub = lax.broadcasted_iota(jnp.int32, (SUBLANES, p.shape[1]), 0)
    wrap_prev = jnp.where(sub == 0, pltpu.roll(p[:HALO], 1, 0),
                          pltpu.roll(cur[rows - SUBLANES:], 1, 0))
    wrap_next = jnp.where(sub == SUBLANES - 1, pltpu.roll(p[HALO + rows:], SUBLANES - 1, 0),
                          pltpu.roll(cur[:SUBLANES], SUBLANES - 1, 0))
    prev = jnp.concatenate([wrap_prev, cur[:rows - SUBLANES]], axis=0)
    nxt = jnp.concatenate([cur[SUBLANES:], wrap_next], axis=0)
    return b_ref[...] + prev * w_ref[0:1, :] + cur * w_ref[1:2, :] + nxt * w_ref[2:3, :]


def _slab(ref, idx, n2, nblk):
    return [ref[idx + (pl.ds(i * BLOCK_ROWS + n2 * SUBLANES, SUBLANES), slice(None))]
            for i in range(nblk)]


def _mod_kernel(c_ref, w_ref, b_ref, o_ref):
    c = c_ref[...]
    act = c * jax.nn.sigmoid(c)
    o_ref[0] = jnp.dot(act, w_ref[0], preferred_element_type=F32,
                       precision=lax.Precision.HIGHEST) + b_ref[0]


def _modulation(c, ada_w, ada_b):
    depth, d, nm = ada_w.shape
    b = c.shape[0]
    tn = nm // 4
    return pl.pallas_call(
        _mod_kernel,
        grid=(depth, nm // tn),
        in_specs=[pl.BlockSpec((b, d), lambda l, j: (0, 0)),
                  pl.BlockSpec((1, d, tn), lambda l, j: (l, 0, j)),
                  pl.BlockSpec((1, 1, tn), lambda l, j: (l, 0, j))],
        out_specs=pl.BlockSpec((1, b, tn), lambda l, j: (l, 0, j)),
        out_shape=jax.ShapeDtypeStruct((depth, b, nm), F32),
        compiler_params=_params(("parallel", "parallel")),
        name="adaln_modulation",
    )(c, ada_w, ada_b.reshape(depth, 1, nm))


def _pitch(n2n):
    return 2 * n2n + SLAB_PAD


def _stage1(load_x, t_ref, s1_refs, n2n, rows_out, k_in=None, by_k1=True):
    nk = rows_out // 2
    def body(n2, carry):
        t = t_ref[pl.ds(_aligned(n2 * rows_out, rows_out), rows_out), :]
        if k_in is not None:
            t = t[:, :k_in]
        a = jnp.dot(t, load_x(n2), preferred_element_type=F32)
        for c, s1_ref in enumerate(s1_refs):
            lanes = slice(c * LANES, (c + 1) * LANES)
            if by_k1:
                s1_ref[pl.ds(n2, nk, stride=_pitch(n2n)), :] = a[:nk, lanes]
                s1_ref[pl.ds(n2n + n2, nk, stride=_pitch(n2n)), :] = a[nk:, lanes]
            else:
                s1_ref[pl.ds(n2 * (rows_out + SLAB_PAD), rows_out), :] = a[:, lanes]
        return carry
    _loop(n2n, body)


def _slab_rows(k1, n2n):
    return pl.ds(k1 * _pitch(n2n), 2 * n2n)


def _load_slab_column(s1_ref, k1, n2n):
    return s1_ref[_slab_rows(k1, n2n), :].astype(BF16)


def _filter_kernel(zf_ref, w1_ref, b1_ref, w2_ref, b2_ref, fr_ref, w3f_ref, w3b_ref, dl_ref,
                   tf_ref, m2_ref, kf_ref, h_ref, fwd_ref, bwd_ref, s1_ref, s1b_ref, *, L):
    n2n = CONV_N2
    nh = L // n2n
    n1n = 2 * nh

    @pl.when(pl.program_id(1) == 0)
    def _():
        fr = fr_ref[0]
        h = jnp.sin(fr * (_dot3(zf_ref[...], w1_ref[0]) + b1_ref[0]))
        h_ref[...] = jnp.sin(fr * (_dot3(h, w2_ref[0]) + b2_ref[0])).astype(BF16)

    w3 = jnp.concatenate([w3f_ref[0], w3b_ref[0]], axis=1).astype(BF16)
    hfb = jnp.dot(h_ref[...], w3, preferred_element_type=F32)
    hf, hb = hfb[:, :LANES], hfb[:, LANES:]
    row = lax.broadcasted_iota(jnp.int32, (L, LANES), 0)
    pos = (row & (nh - 1)) * n2n + (row >> (nh.bit_length() - 1))
    t = pos.astype(F32) / float(max(L - 1, 1))
    decay = jnp.exp(-t * dl_ref[...])
    hf = hf * decay
    hb = jnp.where(pos == 0, 0.0, hb * decay)
    norm = (jnp.sum(jnp.abs(hf), axis=0, keepdims=True)
            + jnp.sum(jnp.abs(hb), axis=0, keepdims=True) + FILTER_EPS)
    inv = 1.0 / (norm * float(2 * L))
    fwd_ref[...] = hf * inv
    bwd_ref[...] = hb * inv

    def load_x(n2):
        return jnp.concatenate([fwd_ref[pl.ds(n2 * nh, nh), :],
                                bwd_ref[pl.ds(n2 * nh, nh), :]], axis=1).astype(BF16)
    _stage1(load_x, tf_ref, [s1_ref, s1b_ref], n2n, 2 * n1n, k_in=nh)

    def body(k1, carry):
        a = jnp.concatenate([_load_slab_column(s1_ref, k1, n2n),
                             _load_slab_column(s1b_ref, k1, n2n)], axis=1)
        u = jnp.dot(m2_ref[...], a, preferred_element_type=F32)
        base = _aligned(k1 * 2 * n2n, 2 * n2n)
        kf_ref[0, 0, pl.ds(base, n2n), :] = u[:n2n, :LANES] + u[:n2n, LANES:]
        kf_ref[0, 0, pl.ds(base + n2n, n2n), :] = u[n2n:, :LANES] - u[n2n:, LANES:]
        return carry
    _loop(n1n, body)


def _filter_spectra(L, w1p, b1, w2, b2, freq, w3, dh):
    depth = w1p.shape[0]
    n2n = CONV_N2
    nh = L // n2n
    n1n = 2 * nh
    tf, _, m2, _ = _conv_tables(L)
    zf = _filter_features(L)
    ct = dh // LANES
    fh = FILTER_HIDDEN
    cols = N_DIRS * ct

    lyr = lambda l, i: (l, 0, 0)
    return pl.pallas_call(
        functools.partial(_filter_kernel, L=L),
        grid=(depth, HYENA_ORDER * ct),
        in_specs=[_const_spec(zf.shape),
                  pl.BlockSpec((1, LANES, fh), lyr), pl.BlockSpec((1, 1, fh), lyr),
                  pl.BlockSpec((1, fh, fh), lyr), pl.BlockSpec((1, 1, fh), lyr),
                  pl.BlockSpec((1, 1, fh), lyr),
                  pl.BlockSpec((1, fh, LANES), lambda l, i: (l, 0, (i // ct) * cols + i % ct)),
                  pl.BlockSpec((1, fh, LANES),
                               lambda l, i: (l, 0, (i // ct) * cols + ct + i % ct)),
                  pl.BlockSpec((1, LANES), lambda l, i: (0, i % ct)),
                  _const_spec(tf.shape), _const_spec(m2.shape)],
        out_specs=pl.BlockSpec((1, 1, n1n * 2 * n2n, LANES),
                               lambda l, i: (l, i // ct, 0, i % ct)),
        out_shape=jax.ShapeDtypeStruct((depth, HYENA_ORDER, n1n * 2 * n2n, dh), F32),
        scratch_shapes=[pltpu.VMEM((L, fh), BF16),
                        pltpu.VMEM((L, LANES), F32), pltpu.VMEM((L, LANES), F32),
                        pltpu.VMEM((n1n * _pitch(n2n), LANES), F32),
                        pltpu.VMEM((n1n * _pitch(n2n), LANES), F32)],
        compiler_params=_params(("parallel", "arbitrary")),
        name="hyena_filter_spectrum",
    )(zf, w1p, b1, w2, b2, freq, w3, w3, _decay_rates(dh), tf, m2)


def _conv_kernel(z_ref, g_ref, kf_ref, d_ref, tf_ref, ti_ref, m2_ref, m2i_ref, o_ref, s1_ref,
                 *, L):
    n2n = CONV_N2
    nh = L // n2n
    n1n = 2 * nh
    nblk = L // BLOCK_ROWS

    def load_x(n2):
        return jnp.concatenate(_slab(z_ref, (0, 0, 0), n2, nblk)
                               + _slab(z_ref, (0, 1, 0), n2, nblk), axis=0).astype(BF16)
    pitch = 2 * n1n + SLAB_PAD
    _stage1(load_x, tf_ref, [s1_ref], n2n, 2 * n1n, by_k1=False)

    def column(k1):
        return [pl.ds(k1, n2n, stride=pitch), pl.ds(n1n + k1, n2n, stride=pitch)]

    def mid(j, carry):
        cols = [j * UNROLL + i for i in range(UNROLL)]
        a = [jnp.concatenate([s1_ref[r, :] for r in column(k1)], axis=0).astype(BF16)
             for k1 in cols]
        w = []
        for i in range(0, UNROLL, 2):
            u = jnp.dot(m2_ref[...], jnp.concatenate(a[i:i + 2], axis=1),
                        preferred_element_type=F32)
            v = []
            for h in range(2):
                ur, ui = u[:n2n, h * LANES:(h + 1) * LANES], u[n2n:, h * LANES:(h + 1) * LANES]
                base = _aligned(cols[i + h] * 2 * n2n, 2 * n2n)
                kr = kf_ref[0, 0, pl.ds(base, n2n), :]
                ki = kf_ref[0, 0, pl.ds(base + n2n, n2n), :]
                v.append(jnp.concatenate([ur * kr - ui * ki, ur * ki + ui * kr], axis=0))
            wk = jnp.dot(m2i_ref[...], jnp.concatenate(v, axis=1).astype(BF16),
                         preferred_element_type=F32)
            w += [wk[:, :LANES], wk[:, LANES:]]
        for k1, wk in zip(cols, w):
            re_rows, im_rows = column(k1)
            s1_ref[re_rows, :] = wk[:n2n]
            s1_ref[im_rows, :] = wk[n2n:]
        return carry
    _loop(n1n // UNROLL, mid)

    d = d_ref[0, 0]

    def last(n2, carry):
        w = s1_ref[pl.ds(n2 * pitch, 2 * n1n), :].astype(BF16)
        t = ti_ref[pl.ds(_aligned(n2 * 2 * nh, 2 * nh), 2 * nh), :]
        y = jnp.dot(t, w, preferred_element_type=F32)
        for half in range(2):
            zz = _slab(z_ref, (0, half, 0), n2, nblk)
            gg = _slab(g_ref, (0, half, 0), n2, nblk)
            for i in range(nblk):
                r0 = half * nh + i * SUBLANES
                o_ref[0, half, 0, pl.ds(i * BLOCK_ROWS + n2 * SUBLANES, SUBLANES), :] = (
                    gg[i] * (y[r0:r0 + SUBLANES] + d * zz[i]))
        return carry
    _loop(n2n, last)


def _long_conv(z, z_col0, gate, gate_col0, kf, d, layer, order):
    _, _, bh, L, _ = z.shape
    dh = kf.shape[-1]
    ct = dh // LANES
    n2n = CONV_N2
    nh = L // n2n
    n1n = 2 * nh
    tf, ti, m2, m2i = _conv_tables(L)
    blk = (1, 2, 1, L, LANES)
    return pl.pallas_call(
        functools.partial(_conv_kernel, L=L),
        grid=(ct, bh),
        in_specs=[pl.BlockSpec(blk, lambda c, p: (z_col0 + c, 0, p, 0, 0)),
                  pl.BlockSpec(blk, lambda c, p: (gate_col0 + c, 0, p, 0, 0)),
                  pl.BlockSpec((1, 1, n1n * 2 * n2n, LANES), lambda c, p: (layer, order, 0, c),
                               pipeline_mode=pl.Buffered(1)),
                  pl.BlockSpec((1, 1, 1, LANES), lambda c, p: (layer, order, 0, c)),
                  _const_spec(tf.shape), _const_spec(ti.shape),
                  _const_spec(m2.shape), _const_spec(m2i.shape)],
        out_specs=pl.BlockSpec(blk, lambda c, p: (c, 0, p, 0, 0)),
        out_shape=jax.ShapeDtypeStruct((ct, 2, bh, L, LANES), F32),
        scratch_shapes=[pltpu.VMEM((n2n * (2 * n1n + SLAB_PAD), LANES), F32)],
        compiler_params=_params(("parallel", "parallel")),
        name="hyena_long_conv",
    )(z, gate, kf, d, tf, ti, m2, m2i)


def _fnet_kernel(p_ref, cs_ref, tf_ref, m4_ref, wg_ref, o_ref, a_ref, b_ref, s1_ref, *, L):
    n2n = CONV_N2
    n1n = L // n2n
    nblk = L // BLOCK_ROWS
    nct = p_ref.shape[0]
    for c in range(nct):
        wg = wg_ref[0, c]
        csw = jnp.concatenate(
            [jnp.dot(cs_ref[:, h * LANES:(h + 1) * LANES], wg, preferred_element_type=F32)
             for h in range(2)], axis=1).astype(BF16)
        ab = jnp.dot(p_ref[c, 0].astype(BF16), csw, preferred_element_type=F32)
        a_ref[c] = ab[:, :LANES]
        b_ref[c] = ab[:, LANES:]

    def load_x(n2):
        return jnp.concatenate(
            [jnp.concatenate(_slab(a_ref, (c,), n2, nblk) + _slab(b_ref, (c,), n2, nblk), axis=0)
             for c in range(nct)], axis=1).astype(BF16)
    _stage1(load_x, tf_ref, [s1_ref.at[c] for c in range(nct)], n2n, 2 * n1n)

    def body(k1, carry):
        a = jnp.concatenate([_load_slab_column(s1_ref.at[c], k1, n2n)
                             for c in range(nct)], axis=1)
        y = jnp.dot(m4_ref[...], a, preferred_element_type=F32)
        for c in range(nct):
            for rr in range(n2n // n1n):
                for i in range(nblk):
                    src = rr * n1n + i * SUBLANES
                    dst = i * BLOCK_ROWS + (k1 + n1n * rr) * SUBLANES
                    o_ref[c, 0, pl.ds(dst, SUBLANES), :] = y[src:src + SUBLANES,
                                                             c * LANES:(c + 1) * LANES]
        return carry
    _loop(n1n, body)


def _fnet_mixer(pf, wg, layer):
    ct, b, L, _ = pf.shape
    cg = ct * LANES // FNET_GROUPS
    cs, tf, m4 = _fnet_tables(L, cg)
    n2n = CONV_N2
    tiles = 2 if ct % 2 == 0 else 1
    return pl.pallas_call(
        functools.partial(_fnet_kernel, L=L),
        grid=(b, ct // tiles),
        in_specs=[pl.BlockSpec((tiles, 1, L, LANES), lambda i, c: (c, i, 0, 0)),
                  _const_spec(cs.shape), _const_spec(tf.shape), _const_spec(m4.shape),
                  pl.BlockSpec((1, tiles, LANES, LANES), lambda i, c: (layer, c, 0, 0))],
        out_specs=pl.BlockSpec((tiles, 1, L, LANES), lambda i, c: (c, i, 0, 0)),
        out_shape=jax.ShapeDtypeStruct((ct, b, L, LANES), F32),
        scratch_shapes=[pltpu.VMEM((tiles, L, LANES), F32), pltpu.VMEM((tiles, L, LANES), F32),
                        pltpu.VMEM((tiles, (L // n2n) * _pitch(n2n), LANES), F32)],
        compiler_params=_params(("parallel", "parallel")),
        name="fnet_mixer",
    )(pf, cs, tf, m4, wg)


def _to_block_slab_kernel(x_ref, o_ref):
    for j in range(o_ref.shape[0]):
        flat = o_ref.at[j, 0]
        for r in range(SUBLANES):
            flat[pl.ds(r, CONV_N2, stride=SUBLANES), :] = (
                x_ref[0, r * CONV_N2:(r + 1) * CONV_N2, j * LANES:(j + 1) * LANES])


def _from_block_slab_kernel(x_ref, o_ref):
    for j in range(x_ref.shape[0]):
        flat = x_ref.at[j, 0]
        for r in range(SUBLANES):
            o_ref[0, r * CONV_N2:(r + 1) * CONV_N2, j * LANES:(j + 1) * LANES] = (
                flat[pl.ds(r, CONV_N2, stride=SUBLANES), :])


def _to_block_slab(x):
    b, L, d = x.shape
    return pl.pallas_call(
        _to_block_slab_kernel,
        grid=(b, L // BLOCK_ROWS),
        in_specs=[pl.BlockSpec((1, BLOCK_ROWS, d), lambda b_, i: (b_, i, 0))],
        out_specs=pl.BlockSpec((d // LANES, 1, BLOCK_ROWS, LANES), lambda b_, i: (0, b_, i, 0)),
        out_shape=jax.ShapeDtypeStruct((d // LANES, b, L, LANES), F32),
        compiler_params=_params(("parallel", "parallel")),
        name="to_block_slab_order",
    )(x)


def _from_block_slab(xs):
    nt, b, L, _ = xs.shape
    return pl.pallas_call(
        _from_block_slab_kernel,
        grid=(b, L // BLOCK_ROWS),
        in_specs=[pl.BlockSpec((nt, 1, BLOCK_ROWS, LANES), lambda b_, i: (0, b_, i, 0))],
        out_specs=pl.BlockSpec((1, BLOCK_ROWS, nt * LANES), lambda b_, i: (b_, i, 0)),
        out_shape=jax.ShapeDtypeStruct((b, L, nt * LANES), F32),
        compiler_params=_params(("parallel", "parallel")),
        name="from_block_slab_order",
    )(xs)


def _block_halo_specs(nt, L):
    nslab = L // SUBLANES
    step = BLOCK_ROWS // SUBLANES
    return [pl.BlockSpec((nt, 1, BLOCK_ROWS, LANES), lambda b, i: (0, b, i, 0)),
            pl.BlockSpec((nt, 1, SUBLANES, LANES),
                         lambda b, i: (0, b, jnp.maximum(i * step - 1, 0), 0)),
            pl.BlockSpec((nt, 1, SUBLANES, LANES),
                         lambda b, i: (0, b, jnp.minimum((i + 1) * step, nslab - 1), 0))]


def _inproj_kernel(xm_ref, xp_ref, xn_ref, mod_ref, g_ref, w_ref, sw_ref, sb_ref,
                   u_ref, pf_ref, h_ref, *, tcol):
    i = pl.program_id(1)
    shift, scale = mod_ref[0, 0, 0:1, :], mod_ref[0, 0, 1:2, :]
    chunks = _halo_rows(xm_ref, xp_ref, xn_ref, h_ref, g_ref[0], scale, shift,
                        i == 0, i == pl.num_programs(1) - 1)
    nconv = u_ref.shape[0] * LANES // tcol
    ncol = nconv + pf_ref.shape[0] * LANES // tcol
    ps = {}
    for s in range(ncol + 1):
        if s < ncol:
            ps[s] = jnp.concatenate(
                [jnp.dot(h, w_ref[0, :, pl.ds(s * tcol, tcol)], preferred_element_type=F32)
                 for h in (chunks if s == 0 else [h_ref[...]])], axis=0)
        if s >= 1:
            f = s - 1
            p = ps.pop(f)
            if f < nconv:
                cols = pl.ds(f * tcol, tcol)
                _store_lanes(u_ref, f * tcol // LANES,
                             _dwconv3(p, sw_ref.at[0, :, cols], sb_ref.at[0, :, cols]))
            else:
                _store_lanes(pf_ref, (f - nconv) * tcol // LANES, p[HALO:HALO + BLOCK_ROWS])


def _in_projection(xs, mod, g_pre, w_in, short_w, short_b, layer):
    nt, b, L, _ = xs.shape
    d = nt * LANES
    e = w_in.shape[-1]
    dconv = short_w.shape[-1]
    out_blk = lambda n: pl.BlockSpec((n, 1, BLOCK_ROWS, LANES), lambda b_, i: (0, b_, i, 0))
    return pl.pallas_call(
        functools.partial(_inproj_kernel, tcol=2 * LANES),
        grid=(b, L // BLOCK_ROWS),
        in_specs=_block_halo_specs(nt, L) + [
            pl.BlockSpec((1, 1, N_MOD, d), lambda b_, i: (layer, b_, 0, 0)),
            pl.BlockSpec((1, 1, d), lambda b_, i: (layer, 0, 0)),
            pl.BlockSpec((1, d, e), lambda b_, i: (layer, 0, 0), pipeline_mode=pl.Buffered(1)),
            pl.BlockSpec((1, 3, dconv), lambda b_, i: (layer, 0, 0)),
            pl.BlockSpec((1, 1, dconv), lambda b_, i: (layer, 0, 0))],
        out_specs=[out_blk(dconv // LANES), out_blk((e - dconv) // LANES)],
        out_shape=[jax.ShapeDtypeStruct((dconv // LANES, b, L, LANES), F32),
                   jax.ShapeDtypeStruct(((e - dconv) // LANES, b, L, LANES), F32)],
        scratch_shapes=[pltpu.VMEM((BLOCK_ROWS + 2 * HALO, d), BF16)],
        compiler_params=_params(("parallel", "parallel")),
        name="in_projection",
    )(xs, xs, xs, mod, g_pre, w_in, short_w, short_b)


def _outproj_kernel(x_ref, yh_ref, yf_ref, mod_ref, g_ref, w_ref, o_ref):
    gate = mod_ref[0, 0, 2:3, :]
    g = g_ref[0]
    ycs, ys = {}, {}
    nsub = BLOCK_ROWS // SUB_ROWS
    for s in range(nsub + 2):
        if s < nsub:
            rows = pl.ds(s * SUB_ROWS, SUB_ROWS)
            ycs[s] = jnp.concatenate([_lanes(yh_ref, rows), _lanes(yf_ref, rows)],
                                     axis=1).astype(BF16)
        if 0 <= s - 1 < nsub:
            ys[s - 1] = jnp.dot(ycs.pop(s - 1), w_ref[0], preferred_element_type=F32)
        if 0 <= s - 2 < nsub:
            rows = pl.ds((s - 2) * SUB_ROWS, SUB_ROWS)
            _store_lanes(o_ref, 0, _lanes(x_ref, rows) + gate * _rmsnorm(ys.pop(s - 2), g), rows)


def _out_projection(xs, yh, yf, mod, g_post, w_out, layer):
    nt, b, L, _ = xs.shape
    d = nt * LANES
    blk = lambda n: pl.BlockSpec((n, 1, BLOCK_ROWS, LANES), lambda b_, i: (0, b_, i, 0))
    return pl.pallas_call(
        _outproj_kernel,
        grid=(b, L // BLOCK_ROWS),
        in_specs=[blk(nt), blk(yh.shape[0]), blk(yf.shape[0]),
                  pl.BlockSpec((1, 1, N_MOD, d), lambda b_, i: (layer, b_, 0, 0)),
                  pl.BlockSpec((1, 1, d), lambda b_, i: (layer, 0, 0)),
                  pl.BlockSpec((1, (yh.shape[0] + yf.shape[0]) * LANES, d),
                               lambda b_, i: (layer, 0, 0), pipeline_mode=pl.Buffered(1))],
        out_specs=blk(nt),
        out_shape=jax.ShapeDtypeStruct(xs.shape, F32),
        compiler_params=_params(("parallel", "parallel")),
        name="out_projection",
    )(xs, yh, yf, mod, g_post, w_out)


def _ffn_kernel(xm_ref, xp_ref, xn_ref, mod_ref, gpre_ref, gpost_ref, wu_ref, cw_ref, cb_ref,
                wd_ref, o_ref, h_ref, *, tf, nf):
    i = pl.program_id(1)
    shift, scale = mod_ref[0, 0, 3:4, :], mod_ref[0, 0, 4:5, :]
    chunks = _halo_rows(xm_ref, xp_ref, xn_ref, h_ref, gpre_ref[0], scale, shift,
                        i == 0, i == pl.num_programs(1) - 1)
    dff = nf * tf

    def up(f):
        hs = chunks if f == 0 else [h_ref[...]]
        return [jnp.concatenate([jnp.dot(h, wu_ref[0, :, pl.ds(c0, tf)],
                                         preferred_element_type=F32) for h in hs], axis=0)
                for c0 in (f * tf, dff + f * tf)]

    def gate_act(f, pa, pb):
        a = _dwconv3(pa, cw_ref.at[0, :, pl.ds(f * tf, tf)], cb_ref.at[0, :, pl.ds(f * tf, tf)])
        b = _dwconv3(pb, cw_ref.at[0, :, pl.ds(dff + f * tf, tf)],
                     cb_ref.at[0, :, pl.ds(dff + f * tf, tf)])
        return (0.5 * a * (1.0 + lax.erf(a * (1.0 / math.sqrt(2.0)))) * b).astype(BF16)

    gate = mod_ref[0, 0, 5:6, :]
    ups, acts, acc = {}, {}, None
    for s in range(nf + 2):
        if s < nf:
            ups[s] = up(s)
        if 0 <= s - 1 < nf:
            acts[s - 1] = gate_act(s - 1, *ups.pop(s - 1))
        f = s - 2
        if 0 <= f < nf and (nf - 1 - f) % 2 == 0:
            f0 = max(f - 1, 0)
            act = jnp.concatenate([acts.pop(k) for k in range(f0, f + 1)], axis=1)
            wd = wd_ref[0, pl.ds(f0 * tf, act.shape[1]), :]
            if f < nf - 1:
                y = jnp.dot(act, wd, preferred_element_type=F32)
                acc = y if acc is None else acc + y
            else:
                step = BLOCK_ROWS // NCHUNK
                for c in range(NCHUNK):
                    rows = pl.ds(c * step, step)
                    y = acc[c * step:(c + 1) * step] + jnp.dot(
                        act[c * step:(c + 1) * step], wd, preferred_element_type=F32)
                    _store_lanes(o_ref, 0,
                                 _lanes(xm_ref, rows) + gate * _rmsnorm(y, gpost_ref[0]), rows)


def _ffn(xs, mod, g_pre, g_post, w_up, dw_w, dw_b, w_down, layer, tf):
    nt, b, L, _ = xs.shape
    d = nt * LANES
    dff = w_down.shape[1]
    idx = lambda b_, i: (layer, 0, 0)
    once = dict(pipeline_mode=pl.Buffered(1))
    return pl.pallas_call(
        functools.partial(_ffn_kernel, tf=tf, nf=dff // tf),
        grid=(b, L // BLOCK_ROWS),
        in_specs=_block_halo_specs(nt, L) + [
            pl.BlockSpec((1, 1, N_MOD, d), lambda b_, i: (layer, b_, 0, 0)),
            pl.BlockSpec((1, 1, d), idx), pl.BlockSpec((1, 1, d), idx),
            pl.BlockSpec((1, d, 2 * dff), idx, **once),
            pl.BlockSpec((1, 3, 2 * dff), idx), pl.BlockSpec((1, 1, 2 * dff), idx),
            pl.BlockSpec((1, dff, d), idx, **once)],
        out_specs=pl.BlockSpec((nt, 1, BLOCK_ROWS, LANES), lambda b_, i: (0, b_, i, 0)),
        out_shape=jax.ShapeDtypeStruct(xs.shape, F32),
        scratch_shapes=[pltpu.VMEM((BLOCK_ROWS + 2 * HALO, d), BF16)],
        compiler_params=_params(("parallel", "parallel")),
        name="geglu_ffn",
    )(xs, xs, xs, mod, g_pre, g_post, w_up, dw_w, dw_b, w_down)


def kernel(x, c, ada_w, ada_b, g_mix_pre, g_mix_post, w_in, short_w, short_b, filt_w1, filt_b1,
           filt_w2, filt_b2, filt_freq, filt_w3, hyena_d, fnet_w, w_out, g_ffn_pre, g_ffn_post,
           w_up, dw_w, dw_b, w_down):
    b, L, d = x.shape
    depth = ada_w.shape[0]
    dh = hyena_d.shape[-1]
    df = w_in.shape[-1] - (HYENA_ORDER + 1) * dh
    cg = df // FNET_GROUPS
    nh = L // CONV_N2
    assert b % 2 == 0 and L % BLOCK_ROWS == 0 and CONV_N2 % nh == 0 and nh & (nh - 1) == 0
    assert d % LANES == 0 and dh % LANES == 0 and df % LANES == 0 and LANES % cg == 0
    tf = 2 * LANES

    row = lambda a: a.reshape(depth, 1, a.shape[-1])
    mod = _modulation(c, ada_w, ada_b).reshape(depth, b, N_MOD, d)

    w1p = jnp.pad(filt_w1, ((0, 0), (0, LANES - filt_w1.shape[1]), (0, 0)))
    kf = _filter_spectra(L, w1p, row(filt_b1), filt_w2, row(filt_b2), row(filt_freq),
                         filt_w3, dh)
    hd = hyena_d.reshape(depth, HYENA_ORDER, 1, dh)

    gpt = LANES // cg
    eye = jnp.eye(gpt, dtype=F32)
    wg = jnp.einsum("ltacd,ae->ltaced", fnet_w.reshape(depth, df // LANES, gpt, cg, cg), eye)
    wg = wg.reshape(depth, df // LANES, LANES, LANES).astype(BF16)

    w_in_b = w_in.astype(BF16)
    w_out_b = w_out.astype(BF16)
    w_up_b = w_up.astype(BF16)
    w_down_b = w_down.astype(BF16)
    g_mix_pre, g_mix_post = row(g_mix_pre), row(g_mix_post)
    g_ffn_pre, g_ffn_post = row(g_ffn_pre), row(g_ffn_post)
    short_b3, dw_b3 = row(short_b), row(dw_b)
    nct = dh // LANES

    xs = _to_block_slab(x)
    for l in range(depth):
        u, pf = _in_projection(xs, mod, g_mix_pre, w_in_b, short_w, short_b3, l)
        u5 = u.reshape(u.shape[0], 2, b // 2, L, LANES)
        z2 = _long_conv(u5, 0, u5, nct, kf, hd, l, 0)
        yh = _long_conv(z2, 0, u5, 2 * nct, kf, hd, l, 1).reshape(nct, b, L, LANES)
        yf = _fnet_mixer(pf, wg, l)
        xs = _out_projection(xs, yh, yf, mod, g_mix_post, w_out_b, l)
        xs = _ffn(xs, mod, g_ffn_pre, g_ffn_post, w_up_b, dw_w, dw_b3, w_down_b, l, tf)
    return _from_block_slab(xs)
```

```python
import functools
import math

import numpy as np
import jax
import jax.numpy as jnp
from jax import lax
from jax.experimental import pallas as pl
from jax.experimental.pallas import tpu as pltpu

F32 = jnp.float32
BF16 = jnp.bfloat16

LANES = 128
SUBLANES = 8
HALO = SUBLANES
VMEM_LIMIT = 56 * 1024 * 1024

FNET_GROUPS = 8
HYENA_ORDER = 2
N_DIRS = 2
FILTER_BANDS = 16
FILTER_HIDDEN = 64
DECAY_TARGET = 1e-2
FAST_DECAY_PCT = 0.3
SLOW_DECAY_PCT = 1.5
N_MOD = 6
NORM_EPS = 1e-6
FILTER_EPS = 1e-6

CONV_N2 = 128
BLOCK_ROWS = CONV_N2 * SUBLANES
SLAB_PAD = 8
SUB_ROWS = 256
NCHUNK = 4
UNROLL = 8


def _params(sem, vmem=VMEM_LIMIT):
    return pltpu.CompilerParams(dimension_semantics=sem, vmem_limit_bytes=vmem)


def _const_spec(shape):
    nd = len(shape)
    return pl.BlockSpec(shape, lambda *_: (0,) * nd, pipeline_mode=pl.Buffered(1))


@functools.lru_cache(maxsize=None)
def _conv_tables(L):
    n2n = CONV_N2
    nh = L // n2n
    n1n = 2 * nh
    N = 2 * L
    k1 = np.arange(n1n)[:, None]
    n1 = np.arange(nh)[None, :]
    tf = np.zeros((n2n, 2 * n1n, 2 * nh), np.float64)
    for n2 in range(n2n):
        th = 2.0 * np.pi * ((k1 * (n2n * n1 + n2)) % N) / N
        c, s = np.cos(th), np.sin(th)
        tf[n2] = np.block([[c, s], [-s, c]])
    ti = np.transpose(tf, (0, 2, 1))
    k2 = np.arange(n2n)[:, None]
    m = np.arange(n2n)[None, :]
    th2 = 2.0 * np.pi * ((k2 * m) % n2n) / n2n
    c2, s2 = np.cos(th2), np.sin(th2)
    m2 = np.block([[c2, s2], [-s2, c2]])
    return (jnp.asarray(tf.reshape(n2n * 2 * n1n, 2 * nh), BF16),
            jnp.asarray(ti.reshape(n2n * 2 * nh, 2 * n1n), BF16),
            jnp.asarray(m2, BF16), jnp.asarray(m2.T, BF16))


@functools.lru_cache(maxsize=None)
def _fnet_tables(L, cg):
    n2n = CONV_N2
    n1n = L // n2n
    q = n2n // n1n
    scale = 1.0 / math.sqrt(L * cg)
    j = np.arange(cg)
    thc = 2.0 * np.pi * ((j[:, None] * j[None, :]) % cg) / cg
    reps = LANES // cg
    cc = np.kron(np.eye(reps), np.cos(thc)) * scale
    sc = np.kron(np.eye(reps), np.sin(thc)) * scale
    cs = np.concatenate([cc, -sc], axis=1)
    k1 = np.arange(n1n)[:, None]
    n1 = np.arange(n1n)[None, :]
    tf = np.zeros((n2n, 2 * n1n, 2 * n1n), np.float64)
    for n2 in range(n2n):
        th = 2.0 * np.pi * ((k1 * (n2n * n1 + n2)) % L) / L
        c, s = np.cos(th), np.sin(th)
        tf[n2] = np.block([[c, s], [-s, c]])
    k2 = np.array([q * a + r for r in range(q) for a in range(n1n)])[:, None]
    m = np.arange(n2n)[None, :]
    th2 = 2.0 * np.pi * ((k2 * m) % n2n) / n2n
    m4 = np.concatenate([np.cos(th2), np.sin(th2)], axis=1)
    return (jnp.asarray(cs, BF16), jnp.asarray(tf.reshape(n2n * 2 * n1n, 2 * n1n), BF16),
            jnp.asarray(m4, BF16))


@functools.lru_cache(maxsize=None)
def _filter_features(L):
    pos = np.arange(L, dtype=np.float32)
    bands = np.linspace(1e-4, FILTER_BANDS - 1, FILTER_BANDS, dtype=np.float32)
    t = pos / np.float32(max(L - 1, 1))
    ang = np.float32(2.0 * math.pi / L) * pos[:, None] * bands[None, :]
    z = np.concatenate([t[:, None], np.cos(ang), -np.sin(ang)], axis=-1).astype(np.float32)
    zp = np.zeros((L, LANES), np.float32)
    zp[:, :z.shape[1]] = z
    zp = zp.reshape(L // CONV_N2, CONV_N2, LANES).transpose(1, 0, 2).reshape(L, LANES)
    return jnp.asarray(zp)


def _decay_rates(dh):
    min_decay = math.log(DECAY_TARGET) / SLOW_DECAY_PCT
    max_decay = math.log(DECAY_TARGET) / FAST_DECAY_PCT
    return jnp.abs(jnp.linspace(min_decay, max_decay, dh, dtype=F32)).reshape(1, dh)


def _loop(n, body):
    for i in range(n):
        body(i, 0)


def _aligned(i, m):
    return i if isinstance(i, int) else pl.multiple_of(i, m)


def _dot3(a, b):
    a_hi = a.astype(BF16)
    b_hi = b.astype(BF16)
    a_lo = (a - a_hi.astype(F32)).astype(BF16)
    b_lo = (b - b_hi.astype(F32)).astype(BF16)
    dot = functools.partial(jnp.dot, preferred_element_type=F32)
    return dot(a_hi, b_hi) + (dot(a_hi, b_lo) + dot(a_lo, b_hi))


def _modnorm(x, g, scale, shift):
    ms = jnp.mean(x * x, axis=-1, keepdims=True)
    return (x * lax.rsqrt(ms + NORM_EPS) * g) * (1.0 + scale) + shift


def _rmsnorm(y, g):
    ms = jnp.mean(y * y, axis=-1, keepdims=True)
    return y * lax.rsqrt(ms + NORM_EPS) * g


def _lanes(ref, rows=slice(None)):
    return jnp.concatenate([ref[j, 0, rows, :] for j in range(ref.shape[0])], axis=1)


def _rows(refs, rows=slice(None)):
    out = _lanes(refs[0], rows)
    for ref in refs[1:]:
        out = out + _lanes(ref, rows)
    return out


def _store_lanes(ref, j0, v, rows=slice(None)):
    for j in range(v.shape[1] // LANES):
        ref[j0 + j, 0, rows, :] = v[:, j * LANES:(j + 1) * LANES]


def _halo_rows(xm_refs, xp_refs, xn_refs, h_ref, g, scale, shift, first, last):
    rows = xm_refs[0].shape[2]
    step = rows // NCHUNK
    pad = 2 * HALO
    edges = [0] + [pad - HALO + step * c for c in range(1, NCHUNK)] + [rows]
    chunks, r0 = [], 0
    for c in range(NCHUNK):
        part = [_modnorm(_rows(xm_refs, pl.ds(edges[c], edges[c + 1] - edges[c])),
                         g, scale, shift)]
        if c == 0:
            part.insert(0, _modnorm(_rows(xp_refs), g, scale, shift) * jnp.where(first, 0.0, 1.0))
        if c == NCHUNK - 1:
            part.append(_modnorm(_rows(xn_refs), g, scale, shift) * jnp.where(last, 0.0, 1.0))
        chunk = jnp.concatenate(part, axis=0).astype(BF16) if len(part) > 1 else part[0].astype(BF16)
        h_ref[pl.ds(r0, chunk.shape[0]), :] = chunk
        chunks.append(chunk)
        r0 += chunk.shape[0]
    return chunks


def _dwconv3(p, w_ref, b_ref):
    rows = p.shape[0] - 2 * HALO
    cur = p[HALO:HALO + rows]
    sub = lax.broadcasted_iota(jnp.int32, (SUBLANES, p.shape[1]), 0)
    wrap_prev = jnp.where(sub == 0, pltpu.roll(p[:HALO], 1, 0),
                          pltpu.roll(cur[rows - SUBLANES:], 1, 0))
    wrap_next = jnp.where(sub == SUBLANES - 1, pltpu.roll(p[HALO + rows:], SUBLANES - 1, 0),
                          pltpu.roll(cur[:SUBLANES], SUBLANES - 1, 0))
    prev = jnp.concatenate([wrap_prev, cur[:rows - SUBLANES]], axis=0)
    nxt = jnp.concatenate([cur[SUBLANES:], wrap_next], axis=0)
    return b_ref[...] + prev * w_ref[0:1, :] + cur * w_ref[1:2, :] + nxt * w_ref[2:3, :]


def _slab(ref, idx, n2, nblk):
    return [ref[idx + (pl.ds(i * BLOCK_ROWS + n2 * SUBLANES, SUBLANES), slice(None))]
            for i in range(nblk)]


def _mod_kernel(c_ref, w_ref, b_ref, o_ref):
    c = c_ref[...]
    act = c * jax.nn.sigmoid(c)
    o_ref[0] = jnp.dot(act, w_ref[0], preferred_element_type=F32,
                       precision=lax.Precision.HIGHEST) + b_ref[0]


def _modulation(c, ada_w, ada_b):
    depth, d, nm = ada_w.shape
    b = c.shape[0]
    tn = nm // 4
    return pl.pallas_call(
        _mod_kernel,
        grid=(depth, nm // tn),
        in_specs=[pl.BlockSpec((b, d), lambda l, j: (0, 0)),
                  pl.BlockSpec((1, d, tn), lambda l, j: (l, 0, j)),
                  pl.BlockSpec((1, 1, tn), lambda l, j: (l, 0, j))],
        out_specs=pl.BlockSpec((1, b, tn), lambda l, j: (l, 0, j)),
        out_shape=jax.ShapeDtypeStruct((depth, b, nm), F32),
        compiler_params=_params(("parallel", "parallel")),
        name="adaln_modulation",
    )(c, ada_w, ada_b.reshape(depth, 1, nm))


def _pitch(n2n):
    return 2 * n2n + SLAB_PAD


def _stage1(load_x, t_ref, s1_refs, n2n, rows_out, k_in=None, by_k1=True):
    nk = rows_out // 2
    def body(n2, carry):
        t = t_ref[pl.ds(_aligned(n2 * rows_out, rows_out), rows_out), :]
        if k_in is not None:
            t = t[:, :k_in]
        a = jnp.dot(t, load_x(n2), preferred_element_type=F32)
        for c, s1_ref in enumerate(s1_refs):
            lanes = slice(c * LANES, (c + 1) * LANES)
            if by_k1:
                s1_ref[pl.ds(n2, nk, stride=_pitch(n2n)), :] = a[:nk, lanes]
                s1_ref[pl.ds(n2n + n2, nk, stride=_pitch(n2n)), :] = a[nk:, lanes]
            else:
                s1_ref[pl.ds(n2 * (rows_out + SLAB_PAD), rows_out), :] = a[:, lanes]
        return carry
    _loop(n2n, body)


def _slab_rows(k1, n2n):
    return pl.ds(k1 * _pitch(n2n), 2 * n2n)


def _load_slab_column(s1_ref, k1, n2n):
    return s1_ref[_slab_rows(k1, n2n), :].astype(BF16)


def _filter_kernel(zf_ref, w1_ref, b1_ref, w2_ref, b2_ref, fr_ref, w3f_ref, w3b_ref, dl_ref,
                   tf_ref, m2_ref, kf_ref, h_ref, fwd_ref, bwd_ref, s1_ref, s1b_ref, *, L):
    n2n = CONV_N2
    nh = L // n2n
    n1n = 2 * nh

    @pl.when(pl.program_id(1) == 0)
    def _():
        fr = fr_ref[0]
        h = jnp.sin(fr * (_dot3(zf_ref[...], w1_ref[0]) + b1_ref[0]))
        h_ref[...] = jnp.sin(fr * (_dot3(h, w2_ref[0]) + b2_ref[0])).astype(BF16)

    w3 = jnp.concatenate([w3f_ref[0], w3b_ref[0]], axis=1).astype(BF16)
    hfb = jnp.dot(h_ref[...], w3, preferred_element_type=F32)
    hf, hb = hfb[:, :LANES], hfb[:, LANES:]
    row = lax.broadcasted_iota(jnp.int32, (L, LANES), 0)
    pos = (row & (nh - 1)) * n2n + (row >> (nh.bit_length() - 1))
    t = pos.astype(F32) / float(max(L - 1, 1))
    decay = jnp.exp(-t * dl_ref[...])
    hf = hf * decay
    hb = jnp.where(pos == 0, 0.0, hb * decay)
    norm = (jnp.sum(jnp.abs(hf), axis=0, keepdims=True)
            + jnp.sum(jnp.abs(hb), axis=0, keepdims=True) + FILTER_EPS)
    inv = 1.0 / (norm * float(2 * L))
    fwd_ref[...] = hf * inv
    bwd_ref[...] = hb * inv

    def load_x(n2):
        return jnp.concatenate([fwd_ref[pl.ds(n2 * nh, nh), :],
                                bwd_ref[pl.ds(n2 * nh, nh), :]], axis=1).astype(BF16)
    _stage1(load_x, tf_ref, [s1_ref, s1b_ref], n2n, 2 * n1n, k_in=nh)

    def body(k1, carry):
        a = jnp.concatenate([_load_slab_column(s1_ref, k1, n2n),
                             _load_slab_column(s1b_ref, k1, n2n)], axis=1)
        u = jnp.dot(m2_ref[...], a, preferred_element_type=F32)
        base = _aligned(k1 * 2 * n2n, 2 * n2n)
        kf_ref[0, 0, pl.ds(base, n2n), :] = u[:n2n, :LANES] + u[:n2n, LANES:]
        kf_ref[0, 0, pl.ds(base + n2n, n2n), :] = u[n2n:, :LANES] - u[n2n:, LANES:]
        return carry
    _loop(n1n, body)


def _filter_spectra(L, w1p, b1, w2, b2, freq, w3, dh):
    depth = w1p.shape[0]
    n2n = CONV_N2
    nh = L // n2n
    n1n = 2 * nh
    tf, _, m2, _ = _conv_tables(L)
    zf = _filter_features(L)
    ct = dh // LANES
    fh = FILTER_HIDDEN
    cols = N_DIRS * ct

    lyr = lambda l, i: (l, 0, 0)
    return pl.pallas_call(
        functools.partial(_filter_kernel, L=L),
        grid=(depth, HYENA_ORDER * ct),
        in_specs=[_const_spec(zf.shape),
                  pl.BlockSpec((1, LANES, fh), lyr), pl.BlockSpec((1, 1, fh), lyr),
                  pl.BlockSpec((1, fh, fh), lyr), pl.BlockSpec((1, 1, fh), lyr),
                  pl.BlockSpec((1, 1, fh), lyr),
                  pl.BlockSpec((1, fh, LANES), lambda l, i: (l, 0, (i // ct) * cols + i % ct)),
                  pl.BlockSpec((1, fh, LANES),
                               lambda l, i: (l, 0, (i // ct) * cols + ct + i % ct)),
                  pl.BlockSpec((1, LANES), lambda l, i: (0, i % ct)),
                  _const_spec(tf.shape), _const_spec(m2.shape)],
        out_specs=pl.BlockSpec((1, 1, n1n * 2 * n2n, LANES),
                               lambda l, i: (l, i // ct, 0, i % ct)),
        out_shape=jax.ShapeDtypeStruct((depth, HYENA_ORDER, n1n * 2 * n2n, dh), F32),
        scratch_shapes=[pltpu.VMEM((L, fh), BF16),
                        pltpu.VMEM((L, LANES), F32), pltpu.VMEM((L, LANES), F32),
                        pltpu.VMEM((n1n * _pitch(n2n), LANES), F32),
                        pltpu.VMEM((n1n * _pitch(n2n), LANES), F32)],
        compiler_params=_params(("parallel", "arbitrary")),
        name="hyena_filter_spectrum",
    )(zf, w1p, b1, w2, b2, freq, w3, w3, _decay_rates(dh), tf, m2)


def _conv_kernel(z_ref, g_ref, kf_ref, d_ref, tf_ref, ti_ref, m2_ref, m2i_ref, o_ref, s1_ref,
                 *, L):
    n2n = CONV_N2
    nh = L // n2n
    n1n = 2 * nh
    nblk = L // BLOCK_ROWS

    def load_x(n2):
        return jnp.concatenate(_slab(z_ref, (0, 0, 0), n2, nblk)
                               + _slab(z_ref, (0, 1, 0), n2, nblk), axis=0).astype(BF16)
    pitch = 2 * n1n + SLAB_PAD
    _stage1(load_x, tf_ref, [s1_ref], n2n, 2 * n1n, by_k1=False)

    def column(k1):
        return [pl.ds(k1, n2n, stride=pitch), pl.ds(n1n + k1, n2n, stride=pitch)]

    def mid(j, carry):
        cols = [j * UNROLL + i for i in range(UNROLL)]
        a = [jnp.concatenate([s1_ref[r, :] for r in column(k1)], axis=0).astype(BF16)
             for k1 in cols]
        w = []
        for i in range(0, UNROLL, 2):
            u = jnp.dot(m2_ref[...], jnp.concatenate(a[i:i + 2], axis=1),
                        preferred_element_type=F32)
            v = []
            for h in range(2):
                ur, ui = u[:n2n, h * LANES:(h + 1) * LANES], u[n2n:, h * LANES:(h + 1) * LANES]
                base = _aligned(cols[i + h] * 2 * n2n, 2 * n2n)
                kr = kf_ref[0, 0, pl.ds(base, n2n), :]
                ki = kf_ref[0, 0, pl.ds(base + n2n, n2n), :]
                v.append(jnp.concatenate([ur * kr - ui * ki, ur * ki + ui * kr], axis=0))
            wk = jnp.dot(m2i_ref[...], jnp.concatenate(v, axis=1).astype(BF16),
                         preferred_element_type=F32)
            w += [wk[:, :LANES], wk[:, LANES:]]
        for k1, wk in zip(cols, w):
            re_rows, im_rows = column(k1)
            s1_ref[re_rows, :] = wk[:n2n]
            s1_ref[im_rows, :] = wk[n2n:]
        return carry
    _loop(n1n // UNROLL, mid)

    d = d_ref[0, 0]

    def last(n2, carry):
        w = s1_ref[pl.ds(n2 * pitch, 2 * n1n), :].astype(BF16)
        t = ti_ref[pl.ds(_aligned(n2 * 2 * nh, 2 * nh), 2 * nh), :]
        y = jnp.dot(t, w, preferred_element_type=F32)
        for half in range(2):
            zz = _slab(z_ref, (0, half, 0), n2, nblk)
            gg = _slab(g_ref, (0, half, 0), n2, nblk)
            for i in range(nblk):
                r0 = half * nh + i * SUBLANES
                o_ref[0, half, 0, pl.ds(i * BLOCK_ROWS + n2 * SUBLANES, SUBLANES), :] = (
                    gg[i] * (y[r0:r0 + SUBLANES] + d * zz[i]))
        return carry
    _loop(n2n, last)


def _long_conv(z, z_col0, gate, gate_col0, kf, d, layer, order):
    _, _, bh, L, _ = z.shape
    dh = kf.shape[-1]
    ct = dh // LANES
    n2n = CONV_N2
    nh = L // n2n
    n1n = 2 * nh
    tf, ti, m2, m2i = _conv_tables(L)
    blk = (1, 2, 1, L, LANES)
    return pl.pallas_call(
        functools.partial(_conv_kernel, L=L),
        grid=(ct, bh),
        in_specs=[pl.BlockSpec(blk, lambda c, p: (z_col0 + c, 0, p, 0, 0)),
                  pl.BlockSpec(blk, lambda c, p: (gate_col0 + c, 0, p, 0, 0)),
                  pl.BlockSpec((1, 1, n1n * 2 * n2n, LANES), lambda c, p: (layer, order, 0, c),
                               pipeline_mode=pl.Buffered(1)),
                  pl.BlockSpec((1, 1, 1, LANES), lambda c, p: (layer, order, 0, c)),
                  _const_spec(tf.shape), _const_spec(ti.shape),
                  _const_spec(m2.shape), _const_spec(m2i.shape)],
        out_specs=pl.BlockSpec(blk, lambda c, p: (c, 0, p, 0, 0)),
        out_shape=jax.ShapeDtypeStruct((ct, 2, bh, L, LANES), F32),
        scratch_shapes=[pltpu.VMEM((n2n * (2 * n1n + SLAB_PAD), LANES), F32)],
        compiler_params=_params(("parallel", "parallel")),
        name="hyena_long_conv",
    )(z, gate, kf, d, tf, ti, m2, m2i)


def _fnet_kernel(p_ref, cs_ref, tf_ref, m4_ref, wg_ref, o_ref, a_ref, b_ref, s1_ref, *, L):
    n2n = CONV_N2
    n1n = L // n2n
    nblk = L // BLOCK_ROWS
    nct = p_ref.shape[0]
    for c in range(nct):
        wg = wg_ref[0, c]
        csw = jnp.concatenate(
            [jnp.dot(cs_ref[:, h * LANES:(h + 1) * LANES], wg, preferred_element_type=F32)
             for h in range(2)], axis=1).astype(BF16)
        ab = jnp.dot(p_ref[c, 0].astype(BF16), csw, preferred_element_type=F32)
        a_ref[c] = ab[:, :LANES]
        b_ref[c] = ab[:, LANES:]

    def load_x(n2):
        return jnp.concatenate(
            [jnp.concatenate(_slab(a_ref, (c,), n2, nblk) + _slab(b_ref, (c,), n2, nblk), axis=0)
             for c in range(nct)], axis=1).astype(BF16)
    _stage1(load_x, tf_ref, [s1_ref.at[c] for c in range(nct)], n2n, 2 * n1n)

    def body(k1, carry):
        a = jnp.concatenate([_load_slab_column(s1_ref.at[c], k1, n2n)
                             for c in range(nct)], axis=1)
        y = jnp.dot(m4_ref[...], a, preferred_element_type=F32)
        for c in range(nct):
            for rr in range(n2n // n1n):
                for i in range(nblk):
                    src = rr * n1n + i * SUBLANES
                    dst = i * BLOCK_ROWS + (k1 + n1n * rr) * SUBLANES
                    o_ref[c, 0, pl.ds(dst, SUBLANES), :] = y[src:src + SUBLANES,
                                                             c * LANES:(c + 1) * LANES]
        return carry
    _loop(n1n, body)


def _fnet_mixer(pf, wg, layer):
    ct, b, L, _ = pf.shape
    cg = ct * LANES // FNET_GROUPS
    cs, tf, m4 = _fnet_tables(L, cg)
    n2n = CONV_N2
    tiles = 2 if ct % 2 == 0 else 1
    return pl.pallas_call(
        functools.partial(_fnet_kernel, L=L),
        grid=(b, ct // tiles),
        in_specs=[pl.BlockSpec((tiles, 1, L, LANES), lambda i, c: (c, i, 0, 0)),
                  _const_spec(cs.shape), _const_spec(tf.shape), _const_spec(m4.shape),
                  pl.BlockSpec((1, tiles, LANES, LANES), lambda i, c: (layer, c, 0, 0))],
        out_specs=pl.BlockSpec((tiles, 1, L, LANES), lambda i, c: (c, i, 0, 0)),
        out_shape=jax.ShapeDtypeStruct((ct, b, L, LANES), F32),
        scratch_shapes=[pltpu.VMEM((tiles, L, LANES), F32), pltpu.VMEM((tiles, L, LANES), F32),
                        pltpu.VMEM((tiles, (L // n2n) * _pitch(n2n), LANES), F32)],
        compiler_params=_params(("parallel", "parallel")),
        name="fnet_mixer",
    )(pf, cs, tf, m4, wg)


def _to_block_slab_kernel(x_ref, o_ref):
    for j in range(o_ref.shape[0]):
        flat = o_ref.at[j, 0]
        for r in range(SUBLANES):
            flat[pl.ds(r, CONV_N2, stride=SUBLANES), :] = (
                x_ref[0, r * CONV_N2:(r + 1) * CONV_N2, j * LANES:(j + 1) * LANES])


def _from_block_slab_kernel(x_ref, o_ref):
    for j in range(x_ref.shape[0]):
        flat = x_ref.at[j, 0]
        for r in range(SUBLANES):
            o_ref[0, r * CONV_N2:(r + 1) * CONV_N2, j * LANES:(j + 1) * LANES] = (
                flat[pl.ds(r, CONV_N2, stride=SUBLANES), :])


def _to_block_slab(x):
    b, L, d = x.shape
    return pl.pallas_call(
        _to_block_slab_kernel,
        grid=(b, L // BLOCK_ROWS),
        in_specs=[pl.BlockSpec((1, BLOCK_ROWS, d), lambda b_, i: (b_, i, 0))],
        out_specs=pl.BlockSpec((d // LANES, 1, BLOCK_ROWS, LANES), lambda b_, i: (0, b_, i, 0)),
        out_shape=jax.ShapeDtypeStruct((d // LANES, b, L, LANES), F32),
        compiler_params=_params(("parallel", "parallel")),
        name="to_block_slab_order",
    )(x)


def _from_block_slab(xs):
    nt, b, L, _ = xs.shape
    return pl.pallas_call(
        _from_block_slab_kernel,
        grid=(b, L // BLOCK_ROWS),
        in_specs=[pl.BlockSpec((nt, 1, BLOCK_ROWS, LANES), lambda b_, i: (0, b_, i, 0))],
        out_specs=pl.BlockSpec((1, BLOCK_ROWS, nt * LANES), lambda b_, i: (b_, i, 0)),
        out_shape=jax.ShapeDtypeStruct((b, L, nt * LANES), F32),
        compiler_params=_params(("parallel", "parallel")),
        name="from_block_slab_order",
    )(xs)


def _block_halo_specs(nt, L):
    nslab = L // SUBLANES
    step = BLOCK_ROWS // SUBLANES
    return [pl.BlockSpec((nt, 1, BLOCK_ROWS, LANES), lambda b, i: (0, b, i, 0)),
            pl.BlockSpec((nt, 1, SUBLANES, LANES),
                         lambda b, i: (0, b, jnp.maximum(i * step - 1, 0), 0)),
            pl.BlockSpec((nt, 1, SUBLANES, LANES),
                         lambda b, i: (0, b, jnp.minimum((i + 1) * step, nslab - 1), 0))]


def _inproj_kernel(xm_ref, xp_ref, xn_ref, mod_ref, g_ref, w_ref, sw_ref, sb_ref,
                   u_ref, pf_ref, h_ref, *, tcol):
    i = pl.program_id(1)
    shift, scale = mod_ref[0, 0, 0:1, :], mod_ref[0, 0, 1:2, :]
    chunks = _halo_rows((xm_ref,), (xp_ref,), (xn_ref,), h_ref, g_ref[0], scale, shift,
                        i == 0, i == pl.num_programs(1) - 1)
    nconv = u_ref.shape[0] * LANES // tcol
    ncol = nconv + pf_ref.shape[0] * LANES // tcol
    ps = {}
    for s in range(ncol + 1):
        if s < ncol:
            ps[s] = jnp.concatenate(
                [jnp.dot(h, w_ref[0, :, pl.ds(s * tcol, tcol)], preferred_element_type=F32)
                 for h in (chunks if s == 0 else [h_ref[...]])], axis=0)
        if s >= 1:
            f = s - 1
            p = ps.pop(f)
            if f < nconv:
                cols = pl.ds(f * tcol, tcol)
                _store_lanes(u_ref, f * tcol // LANES,
                             _dwconv3(p, sw_ref.at[0, :, cols], sb_ref.at[0, :, cols]))
            else:
                _store_lanes(pf_ref, (f - nconv) * tcol // LANES, p[HALO:HALO + BLOCK_ROWS])


def _in_projection(xs, mod, g_pre, w_in, short_w, short_b, layer):
    nt, b, L, _ = xs.shape
    d = nt * LANES
    e = w_in.shape[-1]
    dconv = short_w.shape[-1]
    out_blk = lambda n: pl.BlockSpec((n, 1, BLOCK_ROWS, LANES), lambda b_, i: (0, b_, i, 0))
    return pl.pallas_call(
        functools.partial(_inproj_kernel, tcol=2 * LANES),
        grid=(b, L // BLOCK_ROWS),
        in_specs=_block_halo_specs(nt, L) + [
            pl.BlockSpec((1, 1, N_MOD, d), lambda b_, i: (layer, b_, 0, 0)),
            pl.BlockSpec((1, 1, d), lambda b_, i: (layer, 0, 0)),
            pl.BlockSpec((1, d, e), lambda b_, i: (layer, 0, 0), pipeline_mode=pl.Buffered(1)),
            pl.BlockSpec((1, 3, dconv), lambda b_, i: (layer, 0, 0)),
            pl.BlockSpec((1, 1, dconv), lambda b_, i: (layer, 0, 0))],
        out_specs=[out_blk(dconv // LANES), out_blk((e - dconv) // LANES)],
        out_shape=[jax.ShapeDtypeStruct((dconv // LANES, b, L, LANES), F32),
                   jax.ShapeDtypeStruct(((e - dconv) // LANES, b, L, LANES), F32)],
        scratch_shapes=[pltpu.VMEM((BLOCK_ROWS + 2 * HALO, d), BF16)],
        compiler_params=_params(("parallel", "parallel")),
        name="in_projection",
    )(xs, xs, xs, mod, g_pre, w_in, short_w, short_b)


def _outproj_kernel(yh_ref, yf_ref, mod_ref, g_ref, w_ref, o_ref):
    gate = mod_ref[0, 0, 2:3, :]
    g = g_ref[0]
    ycs, ys = {}, {}
    nsub = BLOCK_ROWS // SUB_ROWS
    for s in range(nsub + 2):
        if s < nsub:
            rows = pl.ds(s * SUB_ROWS, SUB_ROWS)
            ycs[s] = jnp.concatenate([_lanes(yh_ref, rows), _lanes(yf_ref, rows)],
                                     axis=1).astype(BF16)
        if 0 <= s - 1 < nsub:
            ys[s - 1] = jnp.dot(ycs.pop(s - 1), w_ref[0], preferred_element_type=F32)
        if 0 <= s - 2 < nsub:
            rows = pl.ds((s - 2) * SUB_ROWS, SUB_ROWS)
            _store_lanes(o_ref, 0, gate * _rmsnorm(ys.pop(s - 2), g), rows)


def _out_projection(yh, yf, mod, g_post, w_out, layer):
    _, b, L, _ = yh.shape
    d = w_out.shape[-1]
    nt = d // LANES
    blk = lambda n: pl.BlockSpec((n, 1, BLOCK_ROWS, LANES), lambda b_, i: (0, b_, i, 0))
    return pl.pallas_call(
        _outproj_kernel,
        grid=(b, L // BLOCK_ROWS),
        in_specs=[blk(yh.shape[0]), blk(yf.shape[0]),
                  pl.BlockSpec((1, 1, N_MOD, d), lambda b_, i: (layer, b_, 0, 0)),
                  pl.BlockSpec((1, 1, d), lambda b_, i: (layer, 0, 0)),
                  pl.BlockSpec((1, (yh.shape[0] + yf.shape[0]) * LANES, d),
                               lambda b_, i: (layer, 0, 0), pipeline_mode=pl.Buffered(1))],
        out_specs=blk(nt),
        out_shape=jax.ShapeDtypeStruct((nt, b, L, LANES), F32),
        compiler_params=_params(("parallel", "parallel")),
        name="out_projection",
    )(yh, yf, mod, g_post, w_out)


def _ffn_kernel(xm_ref, xp_ref, xn_ref, dm_ref, dp_ref, dn_ref, mod_ref, gpre_ref, gpost_ref,
                wu_ref, cw_ref, cb_ref, wd_ref, o_ref, h_ref, *, tf, nf):
    i = pl.program_id(1)
    shift, scale = mod_ref[0, 0, 3:4, :], mod_ref[0, 0, 4:5, :]
    chunks = _halo_rows((xm_ref, dm_ref), (xp_ref, dp_ref), (xn_ref, dn_ref), h_ref, gpre_ref[0],
                        scale, shift, i == 0, i == pl.num_programs(1) - 1)
    dff = nf * tf

    def up(f):
        hs = chunks if f == 0 else [h_ref[...]]
        return [jnp.concatenate([jnp.dot(h, wu_ref[0, :, pl.ds(c0, tf)],
                                         preferred_element_type=F32) for h in hs], axis=0)
                for c0 in (f * tf, dff + f * tf)]

    def gate_act(f, pa, pb):
        a = _dwconv3(pa, cw_ref.at[0, :, pl.ds(f * tf, tf)], cb_ref.at[0, :, pl.ds(f * tf, tf)])
        b = _dwconv3(pb, cw_ref.at[0, :, pl.ds(dff + f * tf, tf)],
                     cb_ref.at[0, :, pl.ds(dff + f * tf, tf)])
        return (0.5 * a * (1.0 + lax.erf(a * (1.0 / math.sqrt(2.0)))) * b).astype(BF16)

    gate = mod_ref[0, 0, 5:6, :]
    ups, acts, acc = {}, {}, None
    for s in range(nf + 2):
        if s < nf:
            ups[s] = up(s)
        if 0 <= s - 1 < nf:
            acts[s - 1] = gate_act(s - 1, *ups.pop(s - 1))
        f = s - 2
        if 0 <= f < nf and (nf - 1 - f) % 2 == 0:
            f0 = max(f - 1, 0)
            act = jnp.concatenate([acts.pop(k) for k in range(f0, f + 1)], axis=1)
            wd = wd_ref[0, pl.ds(f0 * tf, act.shape[1]), :]
            if f < nf - 1:
                y = jnp.dot(act, wd, preferred_element_type=F32)
                acc = y if acc is None else acc + y
            else:
                step = BLOCK_ROWS // NCHUNK
                for c in range(NCHUNK):
                    rows = pl.ds(c * step, step)
                    y = acc[c * step:(c + 1) * step] + jnp.dot(
                        act[c * step:(c + 1) * step], wd, preferred_element_type=F32)
                    _store_lanes(o_ref, 0,
                                 _rows((xm_ref, dm_ref), rows)
                                 + gate * _rmsnorm(y, gpost_ref[0]), rows)


def _ffn(xs, delta, mod, g_pre, g_post, w_up, dw_w, dw_b, w_down, layer, tf):
    nt, b, L, _ = xs.shape
    d = nt * LANES
    dff = w_down.shape[1]
    idx = lambda b_, i: (layer, 0, 0)
    once = dict(pipeline_mode=pl.Buffered(1))
    return pl.pallas_call(
        functools.partial(_ffn_kernel, tf=tf, nf=dff // tf),
        grid=(b, L // BLOCK_ROWS),
        in_specs=_block_halo_specs(nt, L) + _block_halo_specs(nt, L) + [
            pl.BlockSpec((1, 1, N_MOD, d), lambda b_, i: (layer, b_, 0, 0)),
            pl.BlockSpec((1, 1, d), idx), pl.BlockSpec((1, 1, d), idx),
            pl.BlockSpec((1, d, 2 * dff), idx, **once),
            pl.BlockSpec((1, 3, 2 * dff), idx), pl.BlockSpec((1, 1, 2 * dff), idx),
            pl.BlockSpec((1, dff, d), idx, **once)],
        out_specs=pl.BlockSpec((nt, 1, BLOCK_ROWS, LANES), lambda b_, i: (0, b_, i, 0)),
        out_shape=jax.ShapeDtypeStruct(xs.shape, F32),
        scratch_shapes=[pltpu.VMEM((BLOCK_ROWS + 2 * HALO, d), BF16)],
        compiler_params=_params(("parallel", "parallel")),
        name="geglu_ffn",
    )(xs, xs, xs, delta, delta, delta, mod, g_pre, g_post, w_up, dw_w, dw_b, w_down)


def kernel(x, c, ada_w, ada_b, g_mix_pre, g_mix_post, w_in, short_w, short_b, filt_w1, filt_b1,
           filt_w2, filt_b2, filt_freq, filt_w3, hyena_d, fnet_w, w_out, g_ffn_pre, g_ffn_post,
           w_up, dw_w, dw_b, w_down):
    b, L, d = x.shape
    depth = ada_w.shape[0]
    dh = hyena_d.shape[-1]
    df = w_in.shape[-1] - (HYENA_ORDER + 1) * dh
    cg = df // FNET_GROUPS
    nh = L // CONV_N2
    assert b % 2 == 0 and L % BLOCK_ROWS == 0 and CONV_N2 % nh == 0 and nh & (nh - 1) == 0
    assert d % LANES == 0 and dh % LANES == 0 and df % LANES == 0 and LANES % cg == 0
    tf = 2 * LANES

    row = lambda a: a.reshape(depth, 1, a.shape[-1])
    mod = _modulation(c, ada_w, ada_b).reshape(depth, b, N_MOD, d)

    w1p = jnp.pad(filt_w1, ((0, 0), (0, LANES - filt_w1.shape[1]), (0, 0)))
    kf = _filter_spectra(L, w1p, row(filt_b1), filt_w2, row(filt_b2), row(filt_freq),
                         filt_w3, dh)
    hd = hyena_d.reshape(depth, HYENA_ORDER, 1, dh)

    gpt = LANES // cg
    eye = jnp.eye(gpt, dtype=F32)
    wg = jnp.einsum("ltacd,ae->ltaced", fnet_w.reshape(depth, df // LANES, gpt, cg, cg), eye)
    wg = wg.reshape(depth, df // LANES, LANES, LANES).astype(BF16)

    w_in_b = w_in.astype(BF16)
    w_out_b = w_out.astype(BF16)
    w_up_b = w_up.astype(BF16)
    w_down_b = w_down.astype(BF16)
    g_mix_pre, g_mix_post = row(g_mix_pre), row(g_mix_post)
    g_ffn_pre, g_ffn_post = row(g_ffn_pre), row(g_ffn_post)
    short_b3, dw_b3 = row(short_b), row(dw_b)
    nct = dh // LANES

    xs = _to_block_slab(x)
    for l in range(depth):
        u, pf = _in_projection(xs, mod, g_mix_pre, w_in_b, short_w, short_b3, l)
        u5 = u.reshape(u.shape[0], 2, b // 2, L, LANES)
        z2 = _long_conv(u5, 0, u5, nct, kf, hd, l, 0)
        yh = _long_conv(z2, 0, u5, 2 * nct, kf, hd, l, 1).reshape(nct, b, L, LANES)
        yf = _fnet_mixer(pf, wg, l)
        delta = _out_projection(yh, yf, mod, g_mix_post, w_out_b, l)
        xs = _ffn(xs, delta, mod, g_ffn_pre, g_ffn_post, w_up_b, dw_w, dw_b3, w_down_b, l, tf)
    return _from_block_slab(xs)
```

```python
import functools
import math

import numpy as np
import jax
import jax.numpy as jnp
from jax import lax
from jax.experimental import pallas as pl
from jax.experimental.pallas import tpu as pltpu

F32 = jnp.float32
BF16 = jnp.bfloat16

LANES = 128
SUBLANES = 8
HALO = SUBLANES
VMEM_LIMIT = 56 * 1024 * 1024

FNET_GROUPS = 8
HYENA_ORDER = 2
N_DIRS = 2
FILTER_BANDS = 16
FILTER_HIDDEN = 64
DECAY_TARGET = 1e-2
FAST_DECAY_PCT = 0.3
SLOW_DECAY_PCT = 1.5
N_MOD = 6
NORM_EPS = 1e-6
FILTER_EPS = 1e-6

CONV_N2 = 128
BLOCK_ROWS = CONV_N2 * SUBLANES
SLAB_PAD = 8
SUB_ROWS = 256
NCHUNK = 4
UNROLL = 8


def _params(sem, vmem=VMEM_LIMIT):
    return pltpu.CompilerParams(dimension_semantics=sem, vmem_limit_bytes=vmem)


def _const_spec(shape):
    nd = len(shape)
    return pl.BlockSpec(shape, lambda *_: (0,) * nd, pipeline_mode=pl.Buffered(1))


@functools.lru_cache(maxsize=None)
def _conv_tables(L):
    n2n = CONV_N2
    nh = L // n2n
    n1n = 2 * nh
    N = 2 * L
    k1 = np.arange(n1n)[:, None]
    n1 = np.arange(nh)[None, :]
    tf = np.zeros((n2n, 2 * n1n, 2 * nh), np.float64)
    for n2 in range(n2n):
        th = 2.0 * np.pi * ((k1 * (n2n * n1 + n2)) % N) / N
        c, s = np.cos(th), np.sin(th)
        tf[n2] = np.block([[c, s], [-s, c]])
    ti = np.transpose(tf, (0, 2, 1))
    k2 = np.arange(n2n)[:, None]
    m = np.arange(n2n)[None, :]
    th2 = 2.0 * np.pi * ((k2 * m) % n2n) / n2n
    c2, s2 = np.cos(th2), np.sin(th2)
    m2 = np.block([[c2, s2], [-s2, c2]])
    return (jnp.asarray(tf.reshape(n2n * 2 * n1n, 2 * nh), BF16),
            jnp.asarray(ti.reshape(n2n * 2 * nh, 2 * n1n), BF16),
            jnp.asarray(m2, BF16), jnp.asarray(m2.T, BF16))


@functools.lru_cache(maxsize=None)
def _fnet_tables(L, cg):
    n2n = CONV_N2
    n1n = L // n2n
    q = n2n // n1n
    scale = 1.0 / math.sqrt(L * cg)
    j = np.arange(cg)
    thc = 2.0 * np.pi * ((j[:, None] * j[None, :]) % cg) / cg
    reps = LANES // cg
    cc = np.kron(np.eye(reps), np.cos(thc)) * scale
    sc = np.kron(np.eye(reps), np.sin(thc)) * scale
    cs = np.concatenate([cc, -sc], axis=1)
    k1 = np.arange(n1n)[:, None]
    n1 = np.arange(n1n)[None, :]
    tf = np.zeros((n2n, 2 * n1n, 2 * n1n), np.float64)
    for n2 in range(n2n):
        th = 2.0 * np.pi * ((k1 * (n2n * n1 + n2)) % L) / L
        c, s = np.cos(th), np.sin(th)
        tf[n2] = np.block([[c, s], [-s, c]])
    k2 = np.array([q * a + r for r in range(q) for a in range(n1n)])[:, None]
    m = np.arange(n2n)[None, :]
    th2 = 2.0 * np.pi * ((k2 * m) % n2n) / n2n
    m4 = np.concatenate([np.cos(th2), np.sin(th2)], axis=1)
    return (jnp.asarray(cs, BF16), jnp.asarray(tf.reshape(n2n * 2 * n1n, 2 * n1n), BF16),
            jnp.asarray(m4, BF16))


@functools.lru_cache(maxsize=None)
def _filter_features(L):
    pos = np.arange(L, dtype=np.float32)
    bands = np.linspace(1e-4, FILTER_BANDS - 1, FILTER_BANDS, dtype=np.float32)
    t = pos / np.float32(max(L - 1, 1))
    ang = np.float32(2.0 * math.pi / L) * pos[:, None] * bands[None, :]
    z = np.concatenate([t[:, None], np.cos(ang), -np.sin(ang)], axis=-1).astype(np.float32)
    zp = np.zeros((L, LANES), np.float32)
    zp[:, :z.shape[1]] = z
    zp = zp.reshape(L // CONV_N2, CONV_N2, LANES).transpose(1, 0, 2).reshape(L, LANES)
    return jnp.asarray(zp)


def _decay_rates(dh):
    min_decay = math.log(DECAY_TARGET) / SLOW_DECAY_PCT
    max_decay = math.log(DECAY_TARGET) / FAST_DECAY_PCT
    return jnp.abs(jnp.linspace(min_decay, max_decay, dh, dtype=F32)).reshape(1, dh)


def _loop(n, body):
    for i in range(n):
        body(i, 0)


def _aligned(i, m):
    return i if isinstance(i, int) else pl.multiple_of(i, m)


def _dot3(a, b):
    a_hi = a.astype(BF16)
    b_hi = b.astype(BF16)
    a_lo = (a - a_hi.astype(F32)).astype(BF16)
    b_lo = (b - b_hi.astype(F32)).astype(BF16)
    dot = functools.partial(jnp.dot, preferred_element_type=F32)
    return dot(a_hi, b_hi) + (dot(a_hi, b_lo) + dot(a_lo, b_hi))


def _modnorm(x, g, scale, shift):
    ms = jnp.mean(x * x, axis=-1, keepdims=True)
    return (x * lax.rsqrt(ms + NORM_EPS) * g) * (1.0 + scale) + shift


def _rmsnorm(y, g):
    ms = jnp.mean(y * y, axis=-1, keepdims=True)
    return y * lax.rsqrt(ms + NORM_EPS) * g


def _lanes(ref, rows=slice(None)):
    if len(ref.shape) == 3:
        return ref[0, rows, :]
    return jnp.concatenate([ref[j, 0, rows, :] for j in range(ref.shape[0])], axis=1)


def _rows(refs, rows=slice(None)):
    out = _lanes(refs[0], rows)
    for ref in refs[1:]:
        out = out + _lanes(ref, rows)
    return out


def _store_lanes(ref, j0, v, rows=slice(None)):
    for j in range(v.shape[1] // LANES):
        ref[j0 + j, 0, rows, :] = v[:, j * LANES:(j + 1) * LANES]


def _halo_rows(xm_refs, xp_refs, xn_refs, h_ref, g, scale, shift, first, last):
    rows = xm_refs[0].shape[2]
    step = rows // NCHUNK
    pad = 2 * HALO
    edges = [0] + [pad - HALO + step * c for c in range(1, NCHUNK)] + [rows]
    chunks, r0 = [], 0
    for c in range(NCHUNK):
        part = [_modnorm(_rows(xm_refs, pl.ds(edges[c], edges[c + 1] - edges[c])),
                         g, scale, shift)]
        if c == 0:
            part.insert(0, _modnorm(_rows(xp_refs), g, scale, shift) * jnp.where(first, 0.0, 1.0))
        if c == NCHUNK - 1:
            part.append(_modnorm(_rows(xn_refs), g, scale, shift) * jnp.where(last, 0.0, 1.0))
        chunk = jnp.concatenate(part, axis=0).astype(BF16) if len(part) > 1 else part[0].astype(BF16)
        h_ref[pl.ds(r0, chunk.shape[0]), :] = chunk
        chunks.append(chunk)
        r0 += chunk.shape[0]
    return chunks


def _dwconv3(p, w_ref, b_ref):
    rows = p.shape[0] - 2 * HALO
    cur = p[HALO:HALO + rows]
    sub = lax.broadcasted_iota(jnp.int32, (SUBLANES, p.shape[1]), 0)
    wrap_prev = jnp.where(sub == 0, pltpu.roll(p[:HALO], 1, 0),
                          pltpu.roll(cur[rows - SUBLANES:], 1, 0))
    wrap_next = jnp.where(sub == SUBLANES - 1, pltpu.roll(p[HALO + rows:], SUBLANES - 1, 0),
                          pltpu.roll(cur[:SUBLANES], SUBLANES - 1, 0))
    prev = jnp.concatenate([wrap_prev, cur[:rows - SUBLANES]], axis=0)
    nxt = jnp.concatenate([cur[SUBLANES:], wrap_next], axis=0)
    return b_ref[...] + prev * w_ref[0:1, :] + cur * w_ref[1:2, :] + nxt * w_ref[2:3, :]


def _slab(ref, idx, n2, nblk):
    return [ref[idx + (pl.ds(i * BLOCK_ROWS + n2 * SUBLANES, SUBLANES), slice(None))]
            for i in range(nblk)]


def _mod_kernel(c_ref, w_ref, b_ref, o_ref):
    c = c_ref[...]
    act = c * jax.nn.sigmoid(c)
    o_ref[0] = jnp.dot(act, w_ref[0], preferred_element_type=F32,
                       precision=lax.Precision.HIGHEST) + b_ref[0]


def _modulation(c, ada_w, ada_b):
    depth, d, nm = ada_w.shape
    b = c.shape[0]
    tn = nm // 4
    return pl.pallas_call(
        _mod_kernel,
        grid=(depth, nm // tn),
        in_specs=[pl.BlockSpec((b, d), lambda l, j: (0, 0)),
                  pl.BlockSpec((1, d, tn), lambda l, j: (l, 0, j)),
                  pl.BlockSpec((1, 1, tn), lambda l, j: (l, 0, j))],
        out_specs=pl.BlockSpec((1, b, tn), lambda l, j: (l, 0, j)),
        out_shape=jax.ShapeDtypeStruct((depth, b, nm), F32),
        compiler_params=_params(("parallel", "parallel")),
        name="adaln_modulation",
    )(c, ada_w, ada_b.reshape(depth, 1, nm))


def _pitch(n2n):
    return 2 * n2n + SLAB_PAD


def _stage1(load_x, t_ref, s1_refs, n2n, rows_out, k_in=None, by_k1=True):
    nk = rows_out // 2
    def body(n2, carry):
        t = t_ref[pl.ds(_aligned(n2 * rows_out, rows_out), rows_out), :]
        if k_in is not None:
            t = t[:, :k_in]
        a = jnp.dot(t, load_x(n2), preferred_element_type=F32)
        for c, s1_ref in enumerate(s1_refs):
            lanes = slice(c * LANES, (c + 1) * LANES)
            if by_k1:
                s1_ref[pl.ds(n2, nk, stride=_pitch(n2n)), :] = a[:nk, lanes]
                s1_ref[pl.ds(n2n + n2, nk, stride=_pitch(n2n)), :] = a[nk:, lanes]
            else:
                s1_ref[pl.ds(n2 * (rows_out + SLAB_PAD), rows_out), :] = a[:, lanes]
        return carry
    _loop(n2n, body)


def _slab_rows(k1, n2n):
    return pl.ds(k1 * _pitch(n2n), 2 * n2n)


def _load_slab_column(s1_ref, k1, n2n):
    return s1_ref[_slab_rows(k1, n2n), :].astype(BF16)


def _filter_kernel(zf_ref, w1_ref, b1_ref, w2_ref, b2_ref, fr_ref, w3f_ref, w3b_ref, dl_ref,
                   tf_ref, m2_ref, kf_ref, h_ref, fwd_ref, bwd_ref, s1_ref, s1b_ref, *, L):
    n2n = CONV_N2
    nh = L // n2n
    n1n = 2 * nh

    @pl.when(pl.program_id(1) == 0)
    def _():
        fr = fr_ref[0]
        h = jnp.sin(fr * (_dot3(zf_ref[...], w1_ref[0]) + b1_ref[0]))
        h_ref[...] = jnp.sin(fr * (_dot3(h, w2_ref[0]) + b2_ref[0])).astype(BF16)

    w3 = jnp.concatenate([w3f_ref[0], w3b_ref[0]], axis=1).astype(BF16)
    hfb = jnp.dot(h_ref[...], w3, preferred_element_type=F32)
    hf, hb = hfb[:, :LANES], hfb[:, LANES:]
    row = lax.broadcasted_iota(jnp.int32, (L, LANES), 0)
    pos = (row & (nh - 1)) * n2n + (row >> (nh.bit_length() - 1))
    t = pos.astype(F32) / float(max(L - 1, 1))
    decay = jnp.exp(-t * dl_ref[...])
    hf = hf * decay
    hb = jnp.where(pos == 0, 0.0, hb * decay)
    norm = (jnp.sum(jnp.abs(hf), axis=0, keepdims=True)
            + jnp.sum(jnp.abs(hb), axis=0, keepdims=True) + FILTER_EPS)
    inv = 1.0 / (norm * float(2 * L))
    fwd_ref[...] = hf * inv
    bwd_ref[...] = hb * inv

    def load_x(n2):
        return jnp.concatenate([fwd_ref[pl.ds(n2 * nh, nh), :],
                                bwd_ref[pl.ds(n2 * nh, nh), :]], axis=1).astype(BF16)
    _stage1(load_x, tf_ref, [s1_ref, s1b_ref], n2n, 2 * n1n, k_in=nh)

    def body(k1, carry):
        a = jnp.concatenate([_load_slab_column(s1_ref, k1, n2n),
                             _load_slab_column(s1b_ref, k1, n2n)], axis=1)
        u = jnp.dot(m2_ref[...], a, preferred_element_type=F32)
        base = _aligned(k1 * 2 * n2n, 2 * n2n)
        kf_ref[0, 0, pl.ds(base, n2n), :] = u[:n2n, :LANES] + u[:n2n, LANES:]
        kf_ref[0, 0, pl.ds(base + n2n, n2n), :] = u[n2n:, :LANES] - u[n2n:, LANES:]
        return carry
    _loop(n1n, body)


def _filter_spectra(L, w1p, b1, w2, b2, freq, w3, dh):
    depth = w1p.shape[0]
    n2n = CONV_N2
    nh = L // n2n
    n1n = 2 * nh
    tf, _, m2, _ = _conv_tables(L)
    zf = _filter_features(L)
    ct = dh // LANES
    fh = FILTER_HIDDEN
    cols = N_DIRS * ct

    lyr = lambda l, i: (l, 0, 0)
    return pl.pallas_call(
        functools.partial(_filter_kernel, L=L),
        grid=(depth, HYENA_ORDER * ct),
        in_specs=[_const_spec(zf.shape),
                  pl.BlockSpec((1, LANES, fh), lyr), pl.BlockSpec((1, 1, fh), lyr),
                  pl.BlockSpec((1, fh, fh), lyr), pl.BlockSpec((1, 1, fh), lyr),
                  pl.BlockSpec((1, 1, fh), lyr),
                  pl.BlockSpec((1, fh, LANES), lambda l, i: (l, 0, (i // ct) * cols + i % ct)),
                  pl.BlockSpec((1, fh, LANES),
                               lambda l, i: (l, 0, (i // ct) * cols + ct + i % ct)),
                  pl.BlockSpec((1, LANES), lambda l, i: (0, i % ct)),
                  _const_spec(tf.shape), _const_spec(m2.shape)],
        out_specs=pl.BlockSpec((1, 1, n1n * 2 * n2n, LANES),
                               lambda l, i: (l, i // ct, 0, i % ct)),
        out_shape=jax.ShapeDtypeStruct((depth, HYENA_ORDER, n1n * 2 * n2n, dh), F32),
        scratch_shapes=[pltpu.VMEM((L, fh), BF16),
                        pltpu.VMEM((L, LANES), F32), pltpu.VMEM((L, LANES), F32),
                        pltpu.VMEM((n1n * _pitch(n2n), LANES), F32),
                        pltpu.VMEM((n1n * _pitch(n2n), LANES), F32)],
        compiler_params=_params(("parallel", "arbitrary")),
        name="hyena_filter_spectrum",
    )(zf, w1p, b1, w2, b2, freq, w3, w3, _decay_rates(dh), tf, m2)


def _conv_kernel(z_ref, g_ref, kf_ref, d_ref, tf_ref, ti_ref, m2_ref, m2i_ref, o_ref, s1_ref,
                 *, L):
    n2n = CONV_N2
    nh = L // n2n
    n1n = 2 * nh
    nblk = L // BLOCK_ROWS

    def load_x(n2):
        return jnp.concatenate(_slab(z_ref, (0, 0, 0), n2, nblk)
                               + _slab(z_ref, (0, 1, 0), n2, nblk), axis=0).astype(BF16)
    pitch = 2 * n1n + SLAB_PAD
    _stage1(load_x, tf_ref, [s1_ref], n2n, 2 * n1n, by_k1=False)

    def column(k1):
        return [pl.ds(k1, n2n, stride=pitch), pl.ds(n1n + k1, n2n, stride=pitch)]

    def mid(j, carry):
        cols = [j * UNROLL + i for i in range(UNROLL)]
        a = [jnp.concatenate([s1_ref[r, :] for r in column(k1)], axis=0).astype(BF16)
             for k1 in cols]
        w = []
        for i in range(0, UNROLL, 2):
            u = jnp.dot(m2_ref[...], jnp.concatenate(a[i:i + 2], axis=1),
                        preferred_element_type=F32)
            v = []
            for h in range(2):
                ur, ui = u[:n2n, h * LANES:(h + 1) * LANES], u[n2n:, h * LANES:(h + 1) * LANES]
                base = _aligned(cols[i + h] * 2 * n2n, 2 * n2n)
                kr = kf_ref[0, 0, pl.ds(base, n2n), :]
                ki = kf_ref[0, 0, pl.ds(base + n2n, n2n), :]
                v.append(jnp.concatenate([ur * kr - ui * ki, ur * ki + ui * kr], axis=0))
            wk = jnp.dot(m2i_ref[...], jnp.concatenate(v, axis=1).astype(BF16),
                         preferred_element_type=F32)
            w += [wk[:, :LANES], wk[:, LANES:]]
        for k1, wk in zip(cols, w):
            re_rows, im_rows = column(k1)
            s1_ref[re_rows, :] = wk[:n2n]
            s1_ref[im_rows, :] = wk[n2n:]
        return carry
    _loop(n1n // UNROLL, mid)

    d = d_ref[0, 0]

    def last(n2, carry):
        w = s1_ref[pl.ds(n2 * pitch, 2 * n1n), :].astype(BF16)
        t = ti_ref[pl.ds(_aligned(n2 * 2 * nh, 2 * nh), 2 * nh), :]
        y = jnp.dot(t, w, preferred_element_type=F32)
        for half in range(2):
            zz = _slab(z_ref, (0, half, 0), n2, nblk)
            gg = _slab(g_ref, (0, half, 0), n2, nblk)
            for i in range(nblk):
                r0 = half * nh + i * SUBLANES
                o_ref[0, half, 0, pl.ds(i * BLOCK_ROWS + n2 * SUBLANES, SUBLANES), :] = (
                    gg[i] * (y[r0:r0 + SUBLANES] + d * zz[i]))
        return carry
    _loop(n2n, last)


def _long_conv(z, z_col0, gate, gate_col0, kf, d, layer, order):
    _, _, bh, L, _ = z.shape
    dh = kf.shape[-1]
    ct = dh // LANES
    n2n = CONV_N2
    nh = L // n2n
    n1n = 2 * nh
    tf, ti, m2, m2i = _conv_tables(L)
    blk = (1, 2, 1, L, LANES)
    return pl.pallas_call(
        functools.partial(_conv_kernel, L=L),
        grid=(ct, bh),
        in_specs=[pl.BlockSpec(blk, lambda c, p: (z_col0 + c, 0, p, 0, 0)),
                  pl.BlockSpec(blk, lambda c, p: (gate_col0 + c, 0, p, 0, 0)),
                  pl.BlockSpec((1, 1, n1n * 2 * n2n, LANES), lambda c, p: (layer, order, 0, c),
                               pipeline_mode=pl.Buffered(1)),
                  pl.BlockSpec((1, 1, 1, LANES), lambda c, p: (layer, order, 0, c)),
                  _const_spec(tf.shape), _const_spec(ti.shape),
                  _const_spec(m2.shape), _const_spec(m2i.shape)],
        out_specs=pl.BlockSpec(blk, lambda c, p: (c, 0, p, 0, 0)),
        out_shape=jax.ShapeDtypeStruct((ct, 2, bh, L, LANES), F32),
        scratch_shapes=[pltpu.VMEM((n2n * (2 * n1n + SLAB_PAD), LANES), F32)],
        compiler_params=_params(("parallel", "parallel")),
        name="hyena_long_conv",
    )(z, gate, kf, d, tf, ti, m2, m2i)


def _fnet_kernel(p_ref, cs_ref, tf_ref, m4_ref, wg_ref, o_ref, a_ref, b_ref, s1_ref, *, L):
    n2n = CONV_N2
    n1n = L // n2n
    nblk = L // BLOCK_ROWS
    nct = p_ref.shape[0]
    for c in range(nct):
        wg = wg_ref[0, c]
        csw = jnp.concatenate(
            [jnp.dot(cs_ref[:, h * LANES:(h + 1) * LANES], wg, preferred_element_type=F32)
             for h in range(2)], axis=1).astype(BF16)
        ab = jnp.dot(p_ref[c, 0].astype(BF16), csw, preferred_element_type=F32)
        a_ref[c] = ab[:, :LANES]
        b_ref[c] = ab[:, LANES:]

    def load_x(n2):
        return jnp.concatenate(
            [jnp.concatenate(_slab(a_ref, (c,), n2, nblk) + _slab(b_ref, (c,), n2, nblk), axis=0)
             for c in range(nct)], axis=1).astype(BF16)
    _stage1(load_x, tf_ref, [s1_ref.at[c] for c in range(nct)], n2n, 2 * n1n)

    def body(k1, carry):
        a = jnp.concatenate([_load_slab_column(s1_ref.at[c], k1, n2n)
                             for c in range(nct)], axis=1)
        y = jnp.dot(m4_ref[...], a, preferred_element_type=F32)
        for c in range(nct):
            for rr in range(n2n // n1n):
                for i in range(nblk):
                    src = rr * n1n + i * SUBLANES
                    dst = i * BLOCK_ROWS + (k1 + n1n * rr) * SUBLANES
                    o_ref[c, 0, pl.ds(dst, SUBLANES), :] = y[src:src + SUBLANES,
                                                             c * LANES:(c + 1) * LANES]
        return carry
    _loop(n1n, body)


def _fnet_mixer(pf, wg, layer):
    ct, b, L, _ = pf.shape
    cg = ct * LANES // FNET_GROUPS
    cs, tf, m4 = _fnet_tables(L, cg)
    n2n = CONV_N2
    tiles = 2 if ct % 2 == 0 else 1
    return pl.pallas_call(
        functools.partial(_fnet_kernel, L=L),
        grid=(b, ct // tiles),
        in_specs=[pl.BlockSpec((tiles, 1, L, LANES), lambda i, c: (c, i, 0, 0)),
                  _const_spec(cs.shape), _const_spec(tf.shape), _const_spec(m4.shape),
                  pl.BlockSpec((1, tiles, LANES, LANES), lambda i, c: (layer, c, 0, 0))],
        out_specs=pl.BlockSpec((tiles, 1, L, LANES), lambda i, c: (c, i, 0, 0)),
        out_shape=jax.ShapeDtypeStruct((ct, b, L, LANES), F32),
        scratch_shapes=[pltpu.VMEM((tiles, L, LANES), F32), pltpu.VMEM((tiles, L, LANES), F32),
                        pltpu.VMEM((tiles, (L // n2n) * _pitch(n2n), LANES), F32)],
        compiler_params=_params(("parallel", "parallel")),
        name="fnet_mixer",
    )(pf, cs, tf, m4, wg)


def _to_block_slab_kernel(x_ref, o_ref):
    for j in range(o_ref.shape[0]):
        flat = o_ref.at[j, 0]
        for r in range(SUBLANES):
            flat[pl.ds(r, CONV_N2, stride=SUBLANES), :] = (
                x_ref[0, r * CONV_N2:(r + 1) * CONV_N2, j * LANES:(j + 1) * LANES])


def _block_halo_specs(nt, L):
    nslab = L // SUBLANES
    step = BLOCK_ROWS // SUBLANES
    return [pl.BlockSpec((nt, 1, BLOCK_ROWS, LANES), lambda b, i: (0, b, i, 0)),
            pl.BlockSpec((nt, 1, SUBLANES, LANES),
                         lambda b, i: (0, b, jnp.maximum(i * step - 1, 0), 0)),
            pl.BlockSpec((nt, 1, SUBLANES, LANES),
                         lambda b, i: (0, b, jnp.minimum((i + 1) * step, nslab - 1), 0))]


def _inproj_kernel(xm_ref, xp_ref, xn_ref, mod_ref, g_ref, w_ref, sw_ref, sb_ref,
                   u_ref, pf_ref, *rest, tcol):
    h_ref = rest[-1]
    if len(rest) == 2:
        _to_block_slab_kernel(xm_ref, rest[0])
        xm_ref = rest[0]
    i = pl.program_id(1)
    shift, scale = mod_ref[0, 0, 0:1, :], mod_ref[0, 0, 1:2, :]
    chunks = _halo_rows((xm_ref,), (xp_ref,), (xn_ref,), h_ref, g_ref[0], scale, shift,
                        i == 0, i == pl.num_programs(1) - 1)
    nconv = u_ref.shape[0] * LANES // tcol
    ncol = nconv + pf_ref.shape[0] * LANES // tcol
    ps = {}
    for s in range(ncol + 1):
        if s < ncol:
            ps[s] = jnp.concatenate(
                [jnp.dot(h, w_ref[0, :, pl.ds(s * tcol, tcol)], preferred_element_type=F32)
                 for h in (chunks if s == 0 else [h_ref[...]])], axis=0)
        if s >= 1:
            f = s - 1
            p = ps.pop(f)
            if f < nconv:
                cols = pl.ds(f * tcol, tcol)
                _store_lanes(u_ref, f * tcol // LANES,
                             _dwconv3(p, sw_ref.at[0, :, cols], sb_ref.at[0, :, cols]))
            else:
                _store_lanes(pf_ref, (f - nconv) * tcol // LANES, p[HALO:HALO + BLOCK_ROWS])


def _in_projection(xs, mod, g_pre, w_in, short_w, short_b, layer):
    natural = xs.ndim == 3
    if natural:
        b, L, d = xs.shape
        nt = d // LANES
        nslab = L // SUBLANES
        step = BLOCK_ROWS // SUBLANES
        x_specs = [pl.BlockSpec((1, BLOCK_ROWS, d), lambda b_, i: (b_, i, 0)),
                   pl.BlockSpec((1, SUBLANES, d),
                                lambda b_, i: (b_, jnp.maximum(i * step - 1, 0), 0)),
                   pl.BlockSpec((1, SUBLANES, d),
                                lambda b_, i: (b_, jnp.minimum((i + 1) * step, nslab - 1), 0))]
    else:
        nt, b, L, _ = xs.shape
        d = nt * LANES
        x_specs = _block_halo_specs(nt, L)
    e = w_in.shape[-1]
    dconv = short_w.shape[-1]
    out_blk = lambda n: pl.BlockSpec((n, 1, BLOCK_ROWS, LANES), lambda b_, i: (0, b_, i, 0))
    out_tiles = [dconv // LANES, (e - dconv) // LANES] + ([nt] if natural else [])
    return pl.pallas_call(
        functools.partial(_inproj_kernel, tcol=2 * LANES),
        grid=(b, L // BLOCK_ROWS),
        in_specs=x_specs + [
            pl.BlockSpec((1, 1, N_MOD, d), lambda b_, i: (layer, b_, 0, 0)),
            pl.BlockSpec((1, 1, d), lambda b_, i: (layer, 0, 0)),
            pl.BlockSpec((1, d, e), lambda b_, i: (layer, 0, 0), pipeline_mode=pl.Buffered(1)),
            pl.BlockSpec((1, 3, dconv), lambda b_, i: (layer, 0, 0)),
            pl.BlockSpec((1, 1, dconv), lambda b_, i: (layer, 0, 0))],
        out_specs=[out_blk(n) for n in out_tiles],
        out_shape=[jax.ShapeDtypeStruct((n, b, L, LANES), F32) for n in out_tiles],
        scratch_shapes=[pltpu.VMEM((BLOCK_ROWS + 2 * HALO, d), BF16)],
        compiler_params=_params(("parallel", "parallel")),
        name="in_projection",
    )(xs, xs, xs, mod, g_pre, w_in, short_w, short_b)


def _outproj_kernel(yh_ref, yf_ref, mod_ref, g_ref, w_ref, o_ref):
    gate = mod_ref[0, 0, 2:3, :]
    g = g_ref[0]
    ycs, ys = {}, {}
    nsub = BLOCK_ROWS // SUB_ROWS
    for s in range(nsub + 2):
        if s < nsub:
            rows = pl.ds(s * SUB_ROWS, SUB_ROWS)
            ycs[s] = jnp.concatenate([_lanes(yh_ref, rows), _lanes(yf_ref, rows)],
                                     axis=1).astype(BF16)
        if 0 <= s - 1 < nsub:
            ys[s - 1] = jnp.dot(ycs.pop(s - 1), w_ref[0], preferred_element_type=F32)
        if 0 <= s - 2 < nsub:
            rows = pl.ds((s - 2) * SUB_ROWS, SUB_ROWS)
            _store_lanes(o_ref, 0, gate * _rmsnorm(ys.pop(s - 2), g), rows)


def _out_projection(yh, yf, mod, g_post, w_out, layer):
    _, b, L, _ = yh.shape
    d = w_out.shape[-1]
    nt = d // LANES
    blk = lambda n: pl.BlockSpec((n, 1, BLOCK_ROWS, LANES), lambda b_, i: (0, b_, i, 0))
    return pl.pallas_call(
        _outproj_kernel,
        grid=(b, L // BLOCK_ROWS),
        in_specs=[blk(yh.shape[0]), blk(yf.shape[0]),
                  pl.BlockSpec((1, 1, N_MOD, d), lambda b_, i: (layer, b_, 0, 0)),
                  pl.BlockSpec((1, 1, d), lambda b_, i: (layer, 0, 0)),
                  pl.BlockSpec((1, (yh.shape[0] + yf.shape[0]) * LANES, d),
                               lambda b_, i: (layer, 0, 0), pipeline_mode=pl.Buffered(1))],
        out_specs=blk(nt),
        out_shape=jax.ShapeDtypeStruct((nt, b, L, LANES), F32),
        compiler_params=_params(("parallel", "parallel")),
        name="out_projection",
    )(yh, yf, mod, g_post, w_out)


def _ffn_kernel(xm_ref, xp_ref, xn_ref, dm_ref, dp_ref, dn_ref, mod_ref, gpre_ref, gpost_ref,
                wu_ref, cw_ref, cb_ref, wd_ref, o_ref, h_ref, *nat_ref, tf, nf):
    i = pl.program_id(1)
    shift, scale = mod_ref[0, 0, 3:4, :], mod_ref[0, 0, 4:5, :]
    chunks = _halo_rows((xm_ref, dm_ref), (xp_ref, dp_ref), (xn_ref, dn_ref), h_ref, gpre_ref[0],
                        scale, shift, i == 0, i == pl.num_programs(1) - 1)
    dff = nf * tf

    def up(f):
        hs = chunks if f == 0 else [h_ref[...]]
        return [jnp.concatenate([jnp.dot(h, wu_ref[0, :, pl.ds(c0, tf)],
                                         preferred_element_type=F32) for h in hs], axis=0)
                for c0 in (f * tf, dff + f * tf)]

    def gate_act(f, pa, pb):
        a = _dwconv3(pa, cw_ref.at[0, :, pl.ds(f * tf, tf)], cb_ref.at[0, :, pl.ds(f * tf, tf)])
        b = _dwconv3(pb, cw_ref.at[0, :, pl.ds(dff + f * tf, tf)],
                     cb_ref.at[0, :, pl.ds(dff + f * tf, tf)])
        return (0.5 * a * (1.0 + lax.erf(a * (1.0 / math.sqrt(2.0)))) * b).astype(BF16)

    gate = mod_ref[0, 0, 5:6, :]
    ups, acts, acc = {}, {}, None
    for s in range(nf + 2):
        if s < nf:
            ups[s] = up(s)
        if 0 <= s - 1 < nf:
            acts[s - 1] = gate_act(s - 1, *ups.pop(s - 1))
        f = s - 2
        if 0 <= f < nf and (nf - 1 - f) % 2 == 0:
            f0 = max(f - 1, 0)
            act = jnp.concatenate([acts.pop(k) for k in range(f0, f + 1)], axis=1)
            wd = wd_ref[0, pl.ds(f0 * tf, act.shape[1]), :]
            if f < nf - 1:
                y = jnp.dot(act, wd, preferred_element_type=F32)
                acc = y if acc is None else acc + y
            else:
                step = BLOCK_ROWS // NCHUNK
                for c in range(NCHUNK):
                    rows = pl.ds(c * step, step)
                    y = acc[c * step:(c + 1) * step] + jnp.dot(
                        act[c * step:(c + 1) * step], wd, preferred_element_type=F32)
                    out = _rows((xm_ref, dm_ref), rows) + gate * _rmsnorm(y, gpost_ref[0])
                    if not nat_ref:
                        _store_lanes(o_ref, 0, out, rows)
                        continue
                    _store_lanes(nat_ref[0], 0, out)
                    per_r = step // SUBLANES
                    for j in range(nat_ref[0].shape[0]):
                        for r in range(SUBLANES):
                            o_ref[0, pl.ds(r * CONV_N2 + c * per_r, per_r),
                                  j * LANES:(j + 1) * LANES] = (
                                nat_ref[0].at[j, 0][pl.ds(r, per_r, stride=SUBLANES), :])


def _ffn(xs, delta, mod, g_pre, g_post, w_up, dw_w, dw_b, w_down, layer, tf, natural_out):
    nt, b, L, _ = xs.shape
    d = nt * LANES
    dff = w_down.shape[1]
    idx = lambda b_, i: (layer, 0, 0)
    once = dict(pipeline_mode=pl.Buffered(1))
    scratch = [pltpu.VMEM((BLOCK_ROWS + 2 * HALO, d), BF16)]
    if natural_out:
        out_spec = pl.BlockSpec((1, BLOCK_ROWS, d), lambda b_, i: (b_, i, 0))
        out_shape = jax.ShapeDtypeStruct((b, L, d), F32)
        scratch.append(pltpu.VMEM((nt, 1, BLOCK_ROWS // NCHUNK, LANES), F32))
    else:
        out_spec = pl.BlockSpec((nt, 1, BLOCK_ROWS, LANES), lambda b_, i: (0, b_, i, 0))
        out_shape = jax.ShapeDtypeStruct(xs.shape, F32)
    return pl.pallas_call(
        functools.partial(_ffn_kernel, tf=tf, nf=dff // tf),
        grid=(b, L // BLOCK_ROWS),
        in_specs=_block_halo_specs(nt, L) + _block_halo_specs(nt, L) + [
            pl.BlockSpec((1, 1, N_MOD, d), lambda b_, i: (layer, b_, 0, 0)),
            pl.BlockSpec((1, 1, d), idx), pl.BlockSpec((1, 1, d), idx),
            pl.BlockSpec((1, d, 2 * dff), idx, **once),
            pl.BlockSpec((1, 3, 2 * dff), idx), pl.BlockSpec((1, 1, 2 * dff), idx),
            pl.BlockSpec((1, dff, d), idx, **once)],
        out_specs=out_spec,
        out_shape=out_shape,
        scratch_shapes=scratch,
        compiler_params=_params(("parallel", "parallel")),
        name="geglu_ffn",
    )(xs, xs, xs, delta, delta, delta, mod, g_pre, g_post, w_up, dw_w, dw_b, w_down)


def kernel(x, c, ada_w, ada_b, g_mix_pre, g_mix_post, w_in, short_w, short_b, filt_w1, filt_b1,
           filt_w2, filt_b2, filt_freq, filt_w3, hyena_d, fnet_w, w_out, g_ffn_pre, g_ffn_post,
           w_up, dw_w, dw_b, w_down):
    b, L, d = x.shape
    depth = ada_w.shape[0]
    dh = hyena_d.shape[-1]
    df = w_in.shape[-1] - (HYENA_ORDER + 1) * dh
    cg = df // FNET_GROUPS
    nh = L // CONV_N2
    assert b % 2 == 0 and L % BLOCK_ROWS == 0 and CONV_N2 % nh == 0 and nh & (nh - 1) == 0
    assert d % LANES == 0 and dh % LANES == 0 and df % LANES == 0 and LANES % cg == 0
    tf = 2 * LANES

    row = lambda a: a.reshape(depth, 1, a.shape[-1])
    mod = _modulation(c, ada_w, ada_b).reshape(depth, b, N_MOD, d)

    w1p = jnp.pad(filt_w1, ((0, 0), (0, LANES - filt_w1.shape[1]), (0, 0)))
    kf = _filter_spectra(L, w1p, row(filt_b1), filt_w2, row(filt_b2), row(filt_freq),
                         filt_w3, dh)
    hd = hyena_d.reshape(depth, HYENA_ORDER, 1, dh)

    gpt = LANES // cg
    eye = jnp.eye(gpt, dtype=F32)
    wg = jnp.einsum("ltacd,ae->ltaced", fnet_w.reshape(depth, df // LANES, gpt, cg, cg), eye)
    wg = wg.reshape(depth, df // LANES, LANES, LANES).astype(BF16)

    w_in_b = w_in.astype(BF16)
    w_out_b = w_out.astype(BF16)
    w_up_b = w_up.astype(BF16)
    w_down_b = w_down.astype(BF16)
    g_mix_pre, g_mix_post = row(g_mix_pre), row(g_mix_post)
    g_ffn_pre, g_ffn_post = row(g_ffn_pre), row(g_ffn_post)
    short_b3, dw_b3 = row(short_b), row(dw_b)
    nct = dh // LANES

    xs = x
    for l in range(depth):
        u, pf, *stream = _in_projection(xs, mod, g_mix_pre, w_in_b, short_w, short_b3, l)
        xs = stream[0] if stream else xs
        u5 = u.reshape(u.shape[0], 2, b // 2, L, LANES)
        z2 = _long_conv(u5, 0, u5, nct, kf, hd, l, 0)
        yh = _long_conv(z2, 0, u5, 2 * nct, kf, hd, l, 1).reshape(nct, b, L, LANES)
        yf = _fnet_mixer(pf, wg, l)
        delta = _out_projection(yh, yf, mod, g_mix_post, w_out_b, l)
        xs = _ffn(xs, delta, mod, g_ffn_pre, g_ffn_post, w_up_b, dw_w, dw_b3, w_down_b, l, tf,
                  natural_out=l == depth - 1)
    return xs
```

```python
import functools
import math

import numpy as np
import jax
import jax.numpy as jnp
from jax import lax
from jax.experimental import pallas as pl
from jax.experimental.pallas import tpu as pltpu

F32 = jnp.float32
BF16 = jnp.bfloat16

LANES = 128
SUBLANES = 8
HALO = SUBLANES
VMEM_LIMIT = 56 * 1024 * 1024

FNET_GROUPS = 8
HYENA_ORDER = 2
N_DIRS = 2
FILTER_BANDS = 16
FILTER_HIDDEN = 64
DECAY_TARGET = 1e-2
FAST_DECAY_PCT = 0.3
SLOW_DECAY_PCT = 1.5
N_MOD = 6
NORM_EPS = 1e-6
FILTER_EPS = 1e-6

CONV_N2 = 128
BLOCK_ROWS = CONV_N2 * SUBLANES
SLAB_PAD = 8
SUB_ROWS = 256
NCHUNK = 4
UNROLL = 8


def _params(sem, vmem=VMEM_LIMIT):
    return pltpu.CompilerParams(dimension_semantics=sem, vmem_limit_bytes=vmem)


def _const_spec(shape):
    nd = len(shape)
    return pl.BlockSpec(shape, lambda *_: (0,) * nd, pipeline_mode=pl.Buffered(1))


@functools.lru_cache(maxsize=None)
def _conv_tables(L):
    n2n = CONV_N2
    nh = L // n2n
    n1n = 2 * nh
    N = 2 * L
    k1 = np.arange(n1n)[:, None]
    n1 = np.arange(nh)[None, :]
    tf = np.zeros((n2n, 2 * n1n, 2 * nh), np.float64)
    for n2 in range(n2n):
        th = 2.0 * np.pi * ((k1 * (n2n * n1 + n2)) % N) / N
        c, s = np.cos(th), np.sin(th)
        tf[n2] = np.block([[c, s], [-s, c]])
    ti = np.transpose(tf, (0, 2, 1))
    k2 = np.arange(n2n)[:, None]
    m = np.arange(n2n)[None, :]
    th2 = 2.0 * np.pi * ((k2 * m) % n2n) / n2n
    c2, s2 = np.cos(th2), np.sin(th2)
    m2 = np.block([[c2, s2], [-s2, c2]])
    return (jnp.asarray(tf.reshape(n2n * 2 * n1n, 2 * nh), BF16),
            jnp.asarray(ti.reshape(n2n * 2 * nh, 2 * n1n), BF16),
            jnp.asarray(m2, BF16), jnp.asarray(m2.T, BF16))


@functools.lru_cache(maxsize=None)
def _fnet_tables(L, cg):
    n2n = CONV_N2
    n1n = L // n2n
    q = n2n // n1n
    scale = 1.0 / math.sqrt(L * cg)
    j = np.arange(cg)
    thc = 2.0 * np.pi * ((j[:, None] * j[None, :]) % cg) / cg
    reps = LANES // cg
    cc = np.kron(np.eye(reps), np.cos(thc)) * scale
    sc = np.kron(np.eye(reps), np.sin(thc)) * scale
    cs = np.concatenate([cc, -sc], axis=1)
    k1 = np.arange(n1n)[:, None]
    n1 = np.arange(n1n)[None, :]
    tf = np.zeros((n2n, 2 * n1n, 2 * n1n), np.float64)
    for n2 in range(n2n):
        th = 2.0 * np.pi * ((k1 * (n2n * n1 + n2)) % L) / L
        c, s = np.cos(th), np.sin(th)
        tf[n2] = np.block([[c, s], [-s, c]])
    k2 = np.array([q * a + r for r in range(q) for a in range(n1n)])[:, None]
    m = np.arange(n2n)[None, :]
    th2 = 2.0 * np.pi * ((k2 * m) % n2n) / n2n
    m4 = np.concatenate([np.cos(th2), np.sin(th2)], axis=1)
    return (jnp.asarray(cs, BF16), jnp.asarray(tf.reshape(n2n * 2 * n1n, 2 * n1n), BF16),
            jnp.asarray(m4, BF16))


@functools.lru_cache(maxsize=None)
def _filter_features(L):
    pos = np.arange(L, dtype=np.float32)
    bands = np.linspace(1e-4, FILTER_BANDS - 1, FILTER_BANDS, dtype=np.float32)
    t = pos / np.float32(max(L - 1, 1))
    ang = np.float32(2.0 * math.pi / L) * pos[:, None] * bands[None, :]
    z = np.concatenate([t[:, None], np.cos(ang), -np.sin(ang)], axis=-1).astype(np.float32)
    zp = np.zeros((L, LANES), np.float32)
    zp[:, :z.shape[1]] = z
    zp = zp.reshape(L // CONV_N2, CONV_N2, LANES).transpose(1, 0, 2).reshape(L, LANES)
    return jnp.asarray(zp)


def _decay_rates(dh):
    min_decay = math.log(DECAY_TARGET) / SLOW_DECAY_PCT
    max_decay = math.log(DECAY_TARGET) / FAST_DECAY_PCT
    return jnp.abs(jnp.linspace(min_decay, max_decay, dh, dtype=F32)).reshape(1, dh)


def _loop(n, body):
    for i in range(n):
        body(i, 0)


def _aligned(i, m):
    return i if isinstance(i, int) else pl.multiple_of(i, m)


def _dot3(a, b):
    a_hi = a.astype(BF16)
    b_hi = b.astype(BF16)
    a_lo = (a - a_hi.astype(F32)).astype(BF16)
    b_lo = (b - b_hi.astype(F32)).astype(BF16)
    dot = functools.partial(jnp.dot, preferred_element_type=F32)
    return dot(a_hi, b_hi) + (dot(a_hi, b_lo) + dot(a_lo, b_hi))


def _modnorm(x, g, scale, shift):
    ms = jnp.mean(x * x, axis=-1, keepdims=True)
    return (x * lax.rsqrt(ms + NORM_EPS) * g) * (1.0 + scale) + shift


def _rmsnorm(y, g):
    ms = jnp.mean(y * y, axis=-1, keepdims=True)
    return y * lax.rsqrt(ms + NORM_EPS) * g


def _lanes(ref, rows=slice(None)):
    if len(ref.shape) == 3:
        return ref[0, rows, :]
    return jnp.concatenate([ref[j, 0, rows, :] for j in range(ref.shape[0])], axis=1)


def _rows(refs, rows=slice(None)):
    out = _lanes(refs[0], rows)
    for ref in refs[1:]:
        out = out + _lanes(ref, rows)
    return out


def _store_lanes(ref, j0, v, rows=slice(None)):
    for j in range(v.shape[1] // LANES):
        ref[j0 + j, 0, rows, :] = v[:, j * LANES:(j + 1) * LANES]


def _halo_rows(xm_refs, xp_refs, xn_refs, h_ref, g, scale, shift, first, last):
    rows = xm_refs[0].shape[2]
    step = rows // NCHUNK
    pad = 2 * HALO
    edges = [0] + [pad - HALO + step * c for c in range(1, NCHUNK)] + [rows]
    chunks, r0 = [], 0
    for c in range(NCHUNK):
        part = [_modnorm(_rows(xm_refs, pl.ds(edges[c], edges[c + 1] - edges[c])),
                         g, scale, shift)]
        if c == 0:
            part.insert(0, _modnorm(_rows(xp_refs), g, scale, shift) * jnp.where(first, 0.0, 1.0))
        if c == NCHUNK - 1:
            part.append(_modnorm(_rows(xn_refs), g, scale, shift) * jnp.where(last, 0.0, 1.0))
        chunk = jnp.concatenate(part, axis=0).astype(BF16) if len(part) > 1 else part[0].astype(BF16)
        h_ref[pl.ds(r0, chunk.shape[0]), :] = chunk
        chunks.append(chunk)
        r0 += chunk.shape[0]
    return chunks


def _dwconv3(p, w_ref, b_ref):
    rows = p.shape[0] - 2 * HALO
    cur = p[HALO:HALO + rows]
    sub = lax.broadcasted_iota(jnp.int32, (SUBLANES, p.shape[1]), 0)
    wrap_prev = jnp.where(sub == 0, pltpu.roll(p[:HALO], 1, 0),
                          pltpu.roll(cur[rows - SUBLANES:], 1, 0))
    wrap_next = jnp.where(sub == SUBLANES - 1, pltpu.roll(p[HALO + rows:], SUBLANES - 1, 0),
                          pltpu.roll(cur[:SUBLANES], SUBLANES - 1, 0))
    prev = jnp.concatenate([wrap_prev, cur[:rows - SUBLANES]], axis=0)
    nxt = jnp.concatenate([cur[SUBLANES:], wrap_next], axis=0)
    return b_ref[...] + prev * w_ref[0:1, :] + cur * w_ref[1:2, :] + nxt * w_ref[2:3, :]


def _slab(ref, idx, n2, nblk):
    return [ref[idx + (pl.ds(i * BLOCK_ROWS + n2 * SUBLANES, SUBLANES), slice(None))]
            for i in range(nblk)]


def _mod_kernel(c_ref, w_ref, b_ref, o_ref):
    c = c_ref[...]
    act = c * jax.nn.sigmoid(c)
    o_ref[0] = jnp.dot(act, w_ref[0], preferred_element_type=F32,
                       precision=lax.Precision.HIGHEST) + b_ref[0]


def _modulation(c, ada_w, ada_b):
    depth, d, nm = ada_w.shape
    b = c.shape[0]
    tn = nm // 4
    return pl.pallas_call(
        _mod_kernel,
        grid=(depth, nm // tn),
        in_specs=[pl.BlockSpec((b, d), lambda l, j: (0, 0)),
                  pl.BlockSpec((1, d, tn), lambda l, j: (l, 0, j)),
                  pl.BlockSpec((1, 1, tn), lambda l, j: (l, 0, j))],
        out_specs=pl.BlockSpec((1, b, tn), lambda l, j: (l, 0, j)),
        out_shape=jax.ShapeDtypeStruct((depth, b, nm), F32),
        compiler_params=_params(("parallel", "parallel")),
        name="adaln_modulation",
    )(c, ada_w, ada_b.reshape(depth, 1, nm))


def _pitch(n2n):
    return 2 * n2n + SLAB_PAD


def _stage1(load_x, t_ref, s1_refs, n2n, rows_out, k_in=None, by_k1=True):
    nk = rows_out // 2
    def body(n2, carry):
        t = t_ref[pl.ds(_aligned(n2 * rows_out, rows_out), rows_out), :]
        if k_in is not None:
            t = t[:, :k_in]
        a = jnp.dot(t, load_x(n2), preferred_element_type=F32)
        for c, s1_ref in enumerate(s1_refs):
            lanes = slice(c * LANES, (c + 1) * LANES)
            if by_k1:
                s1_ref[pl.ds(n2, nk, stride=_pitch(n2n)), :] = a[:nk, lanes]
                s1_ref[pl.ds(n2n + n2, nk, stride=_pitch(n2n)), :] = a[nk:, lanes]
            else:
                s1_ref[pl.ds(n2 * (rows_out + SLAB_PAD), rows_out), :] = a[:, lanes]
        return carry
    _loop(n2n, body)


def _slab_rows(k1, n2n):
    return pl.ds(k1 * _pitch(n2n), 2 * n2n)


def _load_slab_column(s1_ref, k1, n2n):
    return s1_ref[_slab_rows(k1, n2n), :].astype(BF16)


def _filter_kernel(zf_ref, w1_ref, b1_ref, w2_ref, b2_ref, fr_ref, w3f_ref, w3b_ref, dl_ref,
                   tf_ref, m2_ref, kf_ref, h_ref, fwd_ref, bwd_ref, s1_ref, s1b_ref, *, L):
    n2n = CONV_N2
    nh = L // n2n
    n1n = 2 * nh

    @pl.when(pl.program_id(1) == 0)
    def _():
        fr = fr_ref[0]
        h = jnp.sin(fr * (_dot3(zf_ref[...], w1_ref[0]) + b1_ref[0]))
        h_ref[...] = jnp.sin(fr * (_dot3(h, w2_ref[0]) + b2_ref[0])).astype(BF16)

    w3 = jnp.concatenate([w3f_ref[0], w3b_ref[0]], axis=1).astype(BF16)
    hfb = jnp.dot(h_ref[...], w3, preferred_element_type=F32)
    hf, hb = hfb[:, :LANES], hfb[:, LANES:]
    row = lax.broadcasted_iota(jnp.int32, (L, LANES), 0)
    pos = (row & (nh - 1)) * n2n + (row >> (nh.bit_length() - 1))
    t = pos.astype(F32) / float(max(L - 1, 1))
    decay = jnp.exp(-t * dl_ref[...])
    hf = hf * decay
    hb = jnp.where(pos == 0, 0.0, hb * decay)
    norm = (jnp.sum(jnp.abs(hf), axis=0, keepdims=True)
            + jnp.sum(jnp.abs(hb), axis=0, keepdims=True) + FILTER_EPS)
    inv = 1.0 / (norm * float(2 * L))
    fwd_ref[...] = hf * inv
    bwd_ref[...] = hb * inv

    def load_x(n2):
        return jnp.concatenate([fwd_ref[pl.ds(n2 * nh, nh), :],
                                bwd_ref[pl.ds(n2 * nh, nh), :]], axis=1).astype(BF16)
    _stage1(load_x, tf_ref, [s1_ref, s1b_ref], n2n, 2 * n1n, k_in=nh)

    def body(k1, carry):
        a = jnp.concatenate([_load_slab_column(s1_ref, k1, n2n),
                             _load_slab_column(s1b_ref, k1, n2n)], axis=1)
        u = jnp.dot(m2_ref[...], a, preferred_element_type=F32)
        base = _aligned(k1 * 2 * n2n, 2 * n2n)
        kf_ref[0, 0, pl.ds(base, n2n), :] = (u[:n2n, :LANES] + u[:n2n, LANES:]).astype(BF16)
        kf_ref[0, 0, pl.ds(base + n2n, n2n), :] = (u[n2n:, :LANES] - u[n2n:, LANES:]).astype(BF16)
        return carry
    _loop(n1n, body)


def _filter_spectra(L, w1p, b1, w2, b2, freq, w3, dh):
    depth = w1p.shape[0]
    n2n = CONV_N2
    nh = L // n2n
    n1n = 2 * nh
    tf, _, m2, _ = _conv_tables(L)
    zf = _filter_features(L)
    ct = dh // LANES
    fh = FILTER_HIDDEN
    cols = N_DIRS * ct

    lyr = lambda l, i: (l, 0, 0)
    return pl.pallas_call(
        functools.partial(_filter_kernel, L=L),
        grid=(depth, HYENA_ORDER * ct),
        in_specs=[_const_spec(zf.shape),
                  pl.BlockSpec((1, LANES, fh), lyr), pl.BlockSpec((1, 1, fh), lyr),
                  pl.BlockSpec((1, fh, fh), lyr), pl.BlockSpec((1, 1, fh), lyr),
                  pl.BlockSpec((1, 1, fh), lyr),
                  pl.BlockSpec((1, fh, LANES), lambda l, i: (l, 0, (i // ct) * cols + i % ct)),
                  pl.BlockSpec((1, fh, LANES),
                               lambda l, i: (l, 0, (i // ct) * cols + ct + i % ct)),
                  pl.BlockSpec((1, LANES), lambda l, i: (0, i % ct)),
                  _const_spec(tf.shape), _const_spec(m2.shape)],
        out_specs=pl.BlockSpec((1, 1, n1n * 2 * n2n, LANES),
                               lambda l, i: (l, i // ct, 0, i % ct)),
        out_shape=jax.ShapeDtypeStruct((depth, HYENA_ORDER, n1n * 2 * n2n, dh), BF16),
        scratch_shapes=[pltpu.VMEM((L, fh), BF16),
                        pltpu.VMEM((L, LANES), F32), pltpu.VMEM((L, LANES), F32),
                        pltpu.VMEM((n1n * _pitch(n2n), LANES), F32),
                        pltpu.VMEM((n1n * _pitch(n2n), LANES), F32)],
        compiler_params=_params(("parallel", "arbitrary")),
        name="hyena_filter_spectrum",
    )(zf, w1p, b1, w2, b2, freq, w3, w3, _decay_rates(dh), tf, m2)


def _conv_kernel(z_ref, g_ref, kf_ref, d_ref, tf_ref, ti_ref, m2_ref, m2i_ref, o_ref, s1_ref,
                 *, L):
    n2n = CONV_N2
    nh = L // n2n
    n1n = 2 * nh
    nblk = L // BLOCK_ROWS

    def load_x(n2):
        return jnp.concatenate(_slab(z_ref, (0, 0, 0), n2, nblk)
                               + _slab(z_ref, (0, 1, 0), n2, nblk), axis=0).astype(BF16)
    pitch = 2 * n1n + SLAB_PAD
    _stage1(load_x, tf_ref, [s1_ref], n2n, 2 * n1n, by_k1=False)

    def column(k1):
        return [pl.ds(k1, n2n, stride=pitch), pl.ds(n1n + k1, n2n, stride=pitch)]

    def mid(j, carry):
        cols = [j * UNROLL + i for i in range(UNROLL)]
        a = [jnp.concatenate([s1_ref[r, :] for r in column(k1)], axis=0).astype(BF16)
             for k1 in cols]
        w = []
        for i in range(0, UNROLL, 2):
            u = jnp.dot(m2_ref[...], jnp.concatenate(a[i:i + 2], axis=1),
                        preferred_element_type=F32).astype(BF16)
            v = []
            for h in range(2):
                ur, ui = u[:n2n, h * LANES:(h + 1) * LANES], u[n2n:, h * LANES:(h + 1) * LANES]
                base = _aligned(cols[i + h] * 2 * n2n, 2 * n2n)
                kr = kf_ref[0, 0, pl.ds(base, n2n), :]
                ki = kf_ref[0, 0, pl.ds(base + n2n, n2n), :]
                v.append(jnp.concatenate([ur * kr - ui * ki, ur * ki + ui * kr], axis=0))
            wk = jnp.dot(m2i_ref[...], jnp.concatenate(v, axis=1),
                         preferred_element_type=F32)
            w += [wk[:, :LANES], wk[:, LANES:]]
        for k1, wk in zip(cols, w):
            re_rows, im_rows = column(k1)
            s1_ref[re_rows, :] = wk[:n2n]
            s1_ref[im_rows, :] = wk[n2n:]
        return carry
    _loop(n1n // UNROLL, mid)

    d = d_ref[0, 0]

    def last(n2, carry):
        w = s1_ref[pl.ds(n2 * pitch, 2 * n1n), :].astype(BF16)
        t = ti_ref[pl.ds(_aligned(n2 * 2 * nh, 2 * nh), 2 * nh), :]
        y = jnp.dot(t, w, preferred_element_type=F32)
        for half in range(2):
            zz = _slab(z_ref, (0, half, 0), n2, nblk)
            gg = _slab(g_ref, (0, half, 0), n2, nblk)
            for i in range(nblk):
                r0 = half * nh + i * SUBLANES
                o_ref[0, half, 0, pl.ds(i * BLOCK_ROWS + n2 * SUBLANES, SUBLANES), :] = (
                    gg[i] * (y[r0:r0 + SUBLANES] + d * zz[i]))
        return carry
    _loop(n2n, last)


def _long_conv(z, z_col0, gate, gate_col0, kf, d, layer, order):
    _, _, bh, L, _ = z.shape
    dh = kf.shape[-1]
    ct = dh // LANES
    n2n = CONV_N2
    nh = L // n2n
    n1n = 2 * nh
    tf, ti, m2, m2i = _conv_tables(L)
    blk = (1, 2, 1, L, LANES)
    return pl.pallas_call(
        functools.partial(_conv_kernel, L=L),
        grid=(ct, bh),
        in_specs=[pl.BlockSpec(blk, lambda c, p: (z_col0 + c, 0, p, 0, 0)),
                  pl.BlockSpec(blk, lambda c, p: (gate_col0 + c, 0, p, 0, 0)),
                  pl.BlockSpec((1, 1, n1n * 2 * n2n, LANES), lambda c, p: (layer, order, 0, c),
                               pipeline_mode=pl.Buffered(1)),
                  pl.BlockSpec((1, 1, 1, LANES), lambda c, p: (layer, order, 0, c)),
                  _const_spec(tf.shape), _const_spec(ti.shape),
                  _const_spec(m2.shape), _const_spec(m2i.shape)],
        out_specs=pl.BlockSpec(blk, lambda c, p: (c, 0, p, 0, 0)),
        out_shape=jax.ShapeDtypeStruct((ct, 2, bh, L, LANES), F32),
        scratch_shapes=[pltpu.VMEM((n2n * (2 * n1n + SLAB_PAD), LANES), F32)],
        compiler_params=_params(("parallel", "parallel")),
        name="hyena_long_conv",
    )(z, gate, kf, d, tf, ti, m2, m2i)


def _fnet_kernel(p_ref, cs_ref, tf_ref, m4_ref, wg_ref, o_ref, a_ref, b_ref, s1_ref, *, L):
    n2n = CONV_N2
    n1n = L // n2n
    nblk = L // BLOCK_ROWS
    nct = p_ref.shape[0]
    for c in range(nct):
        wg = wg_ref[0, c]
        csw = jnp.concatenate(
            [jnp.dot(cs_ref[:, h * LANES:(h + 1) * LANES], wg, preferred_element_type=F32)
             for h in range(2)], axis=1).astype(BF16)
        ab = jnp.dot(p_ref[c, 0].astype(BF16), csw, preferred_element_type=F32)
        a_ref[c] = ab[:, :LANES]
        b_ref[c] = ab[:, LANES:]

    def load_x(n2):
        return jnp.concatenate(
            [jnp.concatenate(_slab(a_ref, (c,), n2, nblk) + _slab(b_ref, (c,), n2, nblk), axis=0)
             for c in range(nct)], axis=1).astype(BF16)
    _stage1(load_x, tf_ref, [s1_ref.at[c] for c in range(nct)], n2n, 2 * n1n)

    def body(k1, carry):
        a = jnp.concatenate([_load_slab_column(s1_ref.at[c], k1, n2n)
                             for c in range(nct)], axis=1)
        y = jnp.dot(m4_ref[...], a, preferred_element_type=F32)
        for c in range(nct):
            for rr in range(n2n // n1n):
                for i in range(nblk):
                    src = rr * n1n + i * SUBLANES
                    dst = i * BLOCK_ROWS + (k1 + n1n * rr) * SUBLANES
                    o_ref[c, 0, pl.ds(dst, SUBLANES), :] = y[src:src + SUBLANES,
                                                             c * LANES:(c + 1) * LANES]
        return carry
    _loop(n1n, body)


def _fnet_mixer(pf, wg, layer):
    ct, b, L, _ = pf.shape
    cg = ct * LANES // FNET_GROUPS
    cs, tf, m4 = _fnet_tables(L, cg)
    n2n = CONV_N2
    tiles = 2 if ct % 2 == 0 else 1
    return pl.pallas_call(
        functools.partial(_fnet_kernel, L=L),
        grid=(b, ct // tiles),
        in_specs=[pl.BlockSpec((tiles, 1, L, LANES), lambda i, c: (c, i, 0, 0)),
                  _const_spec(cs.shape), _const_spec(tf.shape), _const_spec(m4.shape),
                  pl.BlockSpec((1, tiles, LANES, LANES), lambda i, c: (layer, c, 0, 0))],
        out_specs=pl.BlockSpec((tiles, 1, L, LANES), lambda i, c: (c, i, 0, 0)),
        out_shape=jax.ShapeDtypeStruct((ct, b, L, LANES), F32),
        scratch_shapes=[pltpu.VMEM((tiles, L, LANES), F32), pltpu.VMEM((tiles, L, LANES), F32),
                        pltpu.VMEM((tiles, (L // n2n) * _pitch(n2n), LANES), F32)],
        compiler_params=_params(("parallel", "parallel")),
        name="fnet_mixer",
    )(pf, cs, tf, m4, wg)


def _to_block_slab_kernel(x_ref, o_ref):
    for j in range(o_ref.shape[0]):
        flat = o_ref.at[j, 0]
        for r in range(SUBLANES):
            flat[pl.ds(r, CONV_N2, stride=SUBLANES), :] = (
                x_ref[0, r * CONV_N2:(r + 1) * CONV_N2, j * LANES:(j + 1) * LANES])


def _block_halo_specs(nt, L):
    nslab = L // SUBLANES
    step = BLOCK_ROWS // SUBLANES
    return [pl.BlockSpec((nt, 1, BLOCK_ROWS, LANES), lambda b, i: (0, b, i, 0)),
            pl.BlockSpec((nt, 1, SUBLANES, LANES),
                         lambda b, i: (0, b, jnp.maximum(i * step - 1, 0), 0)),
            pl.BlockSpec((nt, 1, SUBLANES, LANES),
                         lambda b, i: (0, b, jnp.minimum((i + 1) * step, nslab - 1), 0))]


def _inproj_kernel(xm_ref, xp_ref, xn_ref, mod_ref, g_ref, w_ref, sw_ref, sb_ref,
                   u_ref, pf_ref, *rest, tcol):
    h_ref = rest[-1]
    if len(rest) == 2:
        _to_block_slab_kernel(xm_ref, rest[0])
        xm_ref = rest[0]
    i = pl.program_id(1)
    shift, scale = mod_ref[0, 0, 0:1, :], mod_ref[0, 0, 1:2, :]
    chunks = _halo_rows((xm_ref,), (xp_ref,), (xn_ref,), h_ref, g_ref[0], scale, shift,
                        i == 0, i == pl.num_programs(1) - 1)
    nconv = u_ref.shape[0] * LANES // tcol
    ncol = nconv + pf_ref.shape[0] * LANES // tcol
    ps = {}
    for s in range(ncol + 1):
        if s < ncol:
            ps[s] = jnp.concatenate(
                [jnp.dot(h, w_ref[0, :, pl.ds(s * tcol, tcol)], preferred_element_type=F32)
                 for h in (chunks if s == 0 else [h_ref[...]])], axis=0)
        if s >= 1:
            f = s - 1
            p = ps.pop(f)
            if f < nconv:
                cols = pl.ds(f * tcol, tcol)
                _store_lanes(u_ref, f * tcol // LANES,
                             _dwconv3(p, sw_ref.at[0, :, cols], sb_ref.at[0, :, cols]))
            else:
                _store_lanes(pf_ref, (f - nconv) * tcol // LANES, p[HALO:HALO + BLOCK_ROWS])


def _in_projection(xs, mod, g_pre, w_in, short_w, short_b, layer):
    natural = xs.ndim == 3
    if natural:
        b, L, d = xs.shape
        nt = d // LANES
        nslab = L // SUBLANES
        step = BLOCK_ROWS // SUBLANES
        x_specs = [pl.BlockSpec((1, BLOCK_ROWS, d), lambda b_, i: (b_, i, 0)),
                   pl.BlockSpec((1, SUBLANES, d),
                                lambda b_, i: (b_, jnp.maximum(i * step - 1, 0), 0)),
                   pl.BlockSpec((1, SUBLANES, d),
                                lambda b_, i: (b_, jnp.minimum((i + 1) * step, nslab - 1), 0))]
    else:
        nt, b, L, _ = xs.shape
        d = nt * LANES
        x_specs = _block_halo_specs(nt, L)
    e = w_in.shape[-1]
    dconv = short_w.shape[-1]
    out_blk = lambda n: pl.BlockSpec((n, 1, BLOCK_ROWS, LANES), lambda b_, i: (0, b_, i, 0))
    out_tiles = [dconv // LANES, (e - dconv) // LANES] + ([nt] if natural else [])
    return pl.pallas_call(
        functools.partial(_inproj_kernel, tcol=2 * LANES),
        grid=(b, L // BLOCK_ROWS),
        in_specs=x_specs + [
            pl.BlockSpec((1, 1, N_MOD, d), lambda b_, i: (layer, b_, 0, 0)),
            pl.BlockSpec((1, 1, d), lambda b_, i: (layer, 0, 0)),
            pl.BlockSpec((1, d, e), lambda b_, i: (layer, 0, 0), pipeline_mode=pl.Buffered(1)),
            pl.BlockSpec((1, 3, dconv), lambda b_, i: (layer, 0, 0)),
            pl.BlockSpec((1, 1, dconv), lambda b_, i: (layer, 0, 0))],
        out_specs=[out_blk(n) for n in out_tiles],
        out_shape=[jax.ShapeDtypeStruct((n, b, L, LANES), F32) for n in out_tiles],
        scratch_shapes=[pltpu.VMEM((BLOCK_ROWS + 2 * HALO, d), BF16)],
        compiler_params=_params(("parallel", "parallel")),
        name="in_projection",
    )(xs, xs, xs, mod, g_pre, w_in, short_w, short_b)


def _outproj_kernel(yh_ref, yf_ref, mod_ref, g_ref, w_ref, o_ref):
    gate = mod_ref[0, 0, 2:3, :]
    g = g_ref[0]
    ycs, ys = {}, {}
    nsub = BLOCK_ROWS // SUB_ROWS
    for s in range(nsub + 2):
        if s < nsub:
            rows = pl.ds(s * SUB_ROWS, SUB_ROWS)
            ycs[s] = jnp.concatenate([_lanes(yh_ref, rows), _lanes(yf_ref, rows)],
                                     axis=1).astype(BF16)
        if 0 <= s - 1 < nsub:
            ys[s - 1] = jnp.dot(ycs.pop(s - 1), w_ref[0], preferred_element_type=F32)
        if 0 <= s - 2 < nsub:
            rows = pl.ds((s - 2) * SUB_ROWS, SUB_ROWS)
            _store_lanes(o_ref, 0, gate * _rmsnorm(ys.pop(s - 2), g), rows)


def _out_projection(yh, yf, mod, g_post, w_out, layer):
    _, b, L, _ = yh.shape
    d = w_out.shape[-1]
    nt = d // LANES
    blk = lambda n: pl.BlockSpec((n, 1, BLOCK_ROWS, LANES), lambda b_, i: (0, b_, i, 0))
    return pl.pallas_call(
        _outproj_kernel,
        grid=(b, L // BLOCK_ROWS),
        in_specs=[blk(yh.shape[0]), blk(yf.shape[0]),
                  pl.BlockSpec((1, 1, N_MOD, d), lambda b_, i: (layer, b_, 0, 0)),
                  pl.BlockSpec((1, 1, d), lambda b_, i: (layer, 0, 0)),
                  pl.BlockSpec((1, (yh.shape[0] + yf.shape[0]) * LANES, d),
                               lambda b_, i: (layer, 0, 0), pipeline_mode=pl.Buffered(1))],
        out_specs=blk(nt),
        out_shape=jax.ShapeDtypeStruct((nt, b, L, LANES), F32),
        compiler_params=_params(("parallel", "parallel")),
        name="out_projection",
    )(yh, yf, mod, g_post, w_out)


def _ffn_kernel(xm_ref, xp_ref, xn_ref, dm_ref, dp_ref, dn_ref, mod_ref, gpre_ref, gpost_ref,
                wu_ref, cw_ref, cb_ref, wd_ref, o_ref, h_ref, *nat_ref, tf, nf):
    i = pl.program_id(1)
    shift, scale = mod_ref[0, 0, 3:4, :], mod_ref[0, 0, 4:5, :]
    chunks = _halo_rows((xm_ref, dm_ref), (xp_ref, dp_ref), (xn_ref, dn_ref), h_ref, gpre_ref[0],
                        scale, shift, i == 0, i == pl.num_programs(1) - 1)
    dff = nf * tf

    def up(f):
        hs = chunks if f == 0 else [h_ref[...]]
        return [jnp.concatenate([jnp.dot(h, wu_ref[0, :, pl.ds(c0, tf)],
                                         preferred_element_type=F32) for h in hs], axis=0)
                for c0 in (f * tf, dff + f * tf)]

    def gate_act(f, pa, pb):
        a = _dwconv3(pa, cw_ref.at[0, :, pl.ds(f * tf, tf)], cb_ref.at[0, :, pl.ds(f * tf, tf)])
        b = _dwconv3(pb, cw_ref.at[0, :, pl.ds(dff + f * tf, tf)],
                     cb_ref.at[0, :, pl.ds(dff + f * tf, tf)])
        return (0.5 * a * (1.0 + lax.erf(a * (1.0 / math.sqrt(2.0)))) * b).astype(BF16)

    gate = mod_ref[0, 0, 5:6, :]
    ups, acts, acc = {}, {}, None
    for s in range(nf + 2):
        if s < nf:
            ups[s] = up(s)
        if 0 <= s - 1 < nf:
            acts[s - 1] = gate_act(s - 1, *ups.pop(s - 1))
        f = s - 2
        if 0 <= f < nf and (nf - 1 - f) % 2 == 0:
            f0 = max(f - 1, 0)
            act = jnp.concatenate([acts.pop(k) for k in range(f0, f + 1)], axis=1)
            wd = wd_ref[0, pl.ds(f0 * tf, act.shape[1]), :]
            if f < nf - 1:
                y = jnp.dot(act, wd, preferred_element_type=F32)
                acc = y if acc is None else acc + y
            else:
                step = BLOCK_ROWS // NCHUNK
                for c in range(NCHUNK):
                    rows = pl.ds(c * step, step)
                    y = acc[c * step:(c + 1) * step] + jnp.dot(
                        act[c * step:(c + 1) * step], wd, preferred_element_type=F32)
                    out = _rows((xm_ref, dm_ref), rows) + gate * _rmsnorm(y, gpost_ref[0])
                    if not nat_ref:
                        _store_lanes(o_ref, 0, out, rows)
                        continue
                    _store_lanes(nat_ref[0], 0, out)
                    per_r = step // SUBLANES
                    for j in range(nat_ref[0].shape[0]):
                        for r in range(SUBLANES):
                            o_ref[0, pl.ds(r * CONV_N2 + c * per_r, per_r),
                                  j * LANES:(j + 1) * LANES] = (
                                nat_ref[0].at[j, 0][pl.ds(r, per_r, stride=SUBLANES), :])


def _ffn(xs, delta, mod, g_pre, g_post, w_up, dw_w, dw_b, w_down, layer, tf, natural_out):
    nt, b, L, _ = xs.shape
    d = nt * LANES
    dff = w_down.shape[1]
    idx = lambda b_, i: (layer, 0, 0)
    once = dict(pipeline_mode=pl.Buffered(1))
    scratch = [pltpu.VMEM((BLOCK_ROWS + 2 * HALO, d), BF16)]
    if natural_out:
        out_spec = pl.BlockSpec((1, BLOCK_ROWS, d), lambda b_, i: (b_, i, 0))
        out_shape = jax.ShapeDtypeStruct((b, L, d), F32)
        scratch.append(pltpu.VMEM((nt, 1, BLOCK_ROWS // NCHUNK, LANES), F32))
    else:
        out_spec = pl.BlockSpec((nt, 1, BLOCK_ROWS, LANES), lambda b_, i: (0, b_, i, 0))
        out_shape = jax.ShapeDtypeStruct(xs.shape, F32)
    return pl.pallas_call(
        functools.partial(_ffn_kernel, tf=tf, nf=dff // tf),
        grid=(b, L // BLOCK_ROWS),
        in_specs=_block_halo_specs(nt, L) + _block_halo_specs(nt, L) + [
            pl.BlockSpec((1, 1, N_MOD, d), lambda b_, i: (layer, b_, 0, 0)),
            pl.BlockSpec((1, 1, d), idx), pl.BlockSpec((1, 1, d), idx),
            pl.BlockSpec((1, d, 2 * dff), idx, **once),
            pl.BlockSpec((1, 3, 2 * dff), idx), pl.BlockSpec((1, 1, 2 * dff), idx),
            pl.BlockSpec((1, dff, d), idx, **once)],
        out_specs=out_spec,
        out_shape=out_shape,
        scratch_shapes=scratch,
        compiler_params=_params(("parallel", "parallel")),
        name="geglu_ffn",
    )(xs, xs, xs, delta, delta, delta, mod, g_pre, g_post, w_up, dw_w, dw_b, w_down)


def kernel(x, c, ada_w, ada_b, g_mix_pre, g_mix_post, w_in, short_w, short_b, filt_w1, filt_b1,
           filt_w2, filt_b2, filt_freq, filt_w3, hyena_d, fnet_w, w_out, g_ffn_pre, g_ffn_post,
           w_up, dw_w, dw_b, w_down):
    b, L, d = x.shape
    depth = ada_w.shape[0]
    dh = hyena_d.shape[-1]
    df = w_in.shape[-1] - (HYENA_ORDER + 1) * dh
    cg = df // FNET_GROUPS
    nh = L // CONV_N2
    assert b % 2 == 0 and L % BLOCK_ROWS == 0 and CONV_N2 % nh == 0 and nh & (nh - 1) == 0
    assert d % LANES == 0 and dh % LANES == 0 and df % LANES == 0 and LANES % cg == 0
    tf = 2 * LANES

    row = lambda a: a.reshape(depth, 1, a.shape[-1])
    mod = _modulation(c, ada_w, ada_b).reshape(depth, b, N_MOD, d)

    w1p = jnp.pad(filt_w1, ((0, 0), (0, LANES - filt_w1.shape[1]), (0, 0)))
    kf = _filter_spectra(L, w1p, row(filt_b1), filt_w2, row(filt_b2), row(filt_freq),
                         filt_w3, dh)
    hd = hyena_d.reshape(depth, HYENA_ORDER, 1, dh)

    gpt = LANES // cg
    eye = jnp.eye(gpt, dtype=F32)
    wg = jnp.einsum("ltacd,ae->ltaced", fnet_w.reshape(depth, df // LANES, gpt, cg, cg), eye)
    wg = wg.reshape(depth, df // LANES, LANES, LANES).astype(BF16)

    w_in_b = w_in.astype(BF16)
    w_out_b = w_out.astype(BF16)
    w_up_b = w_up.astype(BF16)
    w_down_b = w_down.astype(BF16)
    g_mix_pre, g_mix_post = row(g_mix_pre), row(g_mix_post)
    g_ffn_pre, g_ffn_post = row(g_ffn_pre), row(g_ffn_post)
    short_b3, dw_b3 = row(short_b), row(dw_b)
    nct = dh // LANES

    xs = x
    for l in range(depth):
        u, pf, *stream = _in_projection(xs, mod, g_mix_pre, w_in_b, short_w, short_b3, l)
        xs = stream[0] if stream else xs
        u5 = u.reshape(u.shape[0], 2, b // 2, L, LANES)
        z2 = _long_conv(u5, 0, u5, nct, kf, hd, l, 0)
        yh = _long_conv(z2, 0, u5, 2 * nct, kf, hd, l, 1).reshape(nct, b, L, LANES)
        yf = _fnet_mixer(pf, wg, l)
        delta = _out_projection(yh, yf, mod, g_mix_post, w_out_b, l)
        xs = _ffn(xs, delta, mod, g_ffn_pre, g_ffn_post, w_up_b, dw_w, dw_b3, w_down_b, l, tf,
                  natural_out=l == depth - 1)
    return xs
```
